```python
import jax, jax.numpy as jnp
from jax import lax
import numpy as np

D_MODEL = 2048
BATCH = 2
SEQ = 16384
DEPTH = 1

PE_DIM = 256
MIX_WIDTH = D_MODEL
ML_WIDTH = MIX_WIDTH // 2
ML_HEADS = 4
ML_DV = ML_WIDTH // ML_HEADS
ML_DQK = ML_DV // 2
ML_QK = ML_HEADS * ML_DQK
ML_CONV = 4
ML_CHUNK = 64
HG_WIDTH = MIX_WIDTH - ML_WIDTH
HG_EXPAND = 128
HG_HEADS = HG_WIDTH // HG_EXPAND
HG_DV = HG_WIDTH // HG_HEADS
HG_FDIM = HG_HEADS * HG_EXPAND
HG_CHUNK = 64
EPS = 1e-6

SPLIT_SIZES = (ML_QK, ML_QK, ML_WIDTH, ML_WIDTH, ML_WIDTH, ML_HEADS, ML_HEADS,
               HG_FDIM, HG_FDIM, HG_WIDTH, HG_WIDTH)
IN_COLS = sum(SPLIT_SIZES)

kernel_name = "hymba_mlstm_hgrn2_block"


def rmsnorm(x, w):
    xf = x.astype(jnp.float32)
    return xf * lax.rsqrt(jnp.mean(xf * xf, axis=-1, keepdims=True) + EPS) * w.astype(jnp.float32)


def causal_dwconv(x, w, b):
    K = w.shape[0]
    S = x.shape[1]
    xp = jnp.pad(x, ((0, 0), (K - 1, 0), (0, 0)))
    y = xp[:, 0:S, :] * w[0]
    for k in range(1, K):
        y = y + xp[:, k:k + S, :] * w[k]
    return y + b


def to_chunks(x, L):
    Bn, S = x.shape[0], x.shape[1]
    x = x.reshape((Bn, S // L, L) + x.shape[2:])
    perm = (1, 0, 3, 2) + tuple(range(4, x.ndim))
    return x.transpose(perm)


def from_chunks(y):
    NC, Bn, H, L, d = y.shape
    return y.transpose(1, 0, 3, 2, 4).reshape(Bn, NC * L, H, d)


def mlstm_chunkwise(q, k, v, i_pre, f_pre):
    L = ML_CHUNK
    Bn, S, H, dqk = q.shape
    dv = v.shape[-1]
    k = k * (dqk ** -0.5)
    logf = jax.nn.log_sigmoid(f_pre)
    xs = (to_chunks(q, L), to_chunks(k, L), to_chunks(v, L), to_chunks(i_pre, L), to_chunks(logf, L))
    causal = jnp.tril(jnp.ones((L, L), dtype=bool))

    def step(carry, inp):
        C, n, m = carry
        qc, kc, vc, ig, lf = inp
        b = jnp.cumsum(lf, axis=-1)
        logD = jnp.where(causal, b[..., :, None] - b[..., None, :] + ig[..., None, :], -jnp.inf)
        inter_log = b + m[..., None]
        m_t = jnp.maximum(inter_log, jnp.max(logD, axis=-1))
        scores = jnp.einsum('bhtd,bhjd->bhtj', qc, kc) * jnp.exp(logD - m_t[..., None])
        w_inter = jnp.exp(inter_log - m_t)
        num = (jnp.einsum('bhtj,bhjv->bhtv', scores, vc)
               + w_inter[..., None] * jnp.einsum('bhtd,bhdv->bhtv', qc, C))
        den = jnp.sum(scores, axis=-1) + w_inter * jnp.einsum('bhtd,bhd->bht', qc, n)
        h = num / jnp.maximum(jnp.abs(den), jnp.exp(-m_t))[..., None]
        g = b[..., -1]
        a = g[..., None] - b + ig
        m_new = jnp.maximum(g + m, jnp.max(a, axis=-1))
        wa = jnp.exp(a - m_new[..., None])
        ws = jnp.exp(g + m - m_new)
        C_new = ws[..., None, None] * C + jnp.einsum('bhj,bhjd,bhjv->bhdv', wa, kc, vc)
        n_new = ws[..., None] * n + jnp.einsum('bhj,bhjd->bhd', wa, kc)
        return (C_new, n_new, m_new), h

    init = (jnp.zeros((Bn, H, dqk, dv), jnp.float32),
            jnp.zeros((Bn, H, dqk), jnp.float32),
            jnp.zeros((Bn, H), jnp.float32))
    _, hs = lax.scan(step, init, xs)
    return from_chunks(hs)


def hgrn2_chunkwise(q, k, logf, iv):
    L = HG_CHUNK
    Bn, S, H, E = q.shape
    dv = iv.shape[-1]
    xs = (to_chunks(q, L), to_chunks(k, L), to_chunks(logf, L), to_chunks(iv, L))
    causal = jnp.tril(jnp.ones((L, L), dtype=bool))[:, :, None]

    def step(Sst, inp):
        qc, kc, lf, vc = inp
        b = jnp.cumsum(lf, axis=2)
        decay = jnp.exp(jnp.where(causal, b[:, :, :, None, :] - b[:, :, None, :, :], -jnp.inf))
        A = jnp.einsum('bhte,bhje,bhtje->bhtj', qc, kc, decay)
        o = (jnp.einsum('bhtj,bhjv->bhtv', A, vc)
             + jnp.einsum('bhte,bhev->bhtv', qc * jnp.exp(b), Sst))
        g = b[:, :, -1, :]
        S_new = (jnp.exp(g)[..., None] * Sst
                 + jnp.einsum('bhje,bhjv->bhev', kc * jnp.exp(g[:, :, None, :] - b), vc))
        return S_new, o

    _, os_ = lax.scan(step, jnp.zeros((Bn, H, E, dv), jnp.float32), xs)
    return from_chunks(os_)


def setup_inputs(seed: int = 0) -> dict:
    key = jax.random.key(seed)
    ks = jax.random.split(key, 16)
    f32 = jnp.float32
    nrm = lambda k, shape, s: jax.random.normal(k, shape, f32) * s
    x = jax.random.normal(ks[0], (BATCH, SEQ, D_MODEL), f32)
    p = jax.random.normal(ks[1], (DEPTH, BATCH, SEQ, PE_DIM), f32)
    norm_w = 1.0 + nrm(ks[2], (DEPTH, D_MODEL), 0.05)
    w_in = nrm(ks[3], (DEPTH, D_MODEL, IN_COLS), D_MODEL ** -0.5)
    conv_w = nrm(ks[4], (DEPTH, ML_CONV, 2 * ML_QK), ML_CONV ** -0.5)
    conv_b = nrm(ks[5], (DEPTH, 2 * ML_QK), 0.02)
    ml_b_i = nrm(ks[6], (DEPTH, ML_HEADS), 0.1)
    ml_b_f = jnp.linspace(3.0, 6.0, ML_HEADS, dtype=f32)[None, :] + nrm(ks[7], (DEPTH, ML_HEADS), 0.1)
    ml_norm_w = 1.0 + nrm(ks[8], (DEPTH, ML_WIDTH), 0.05)
    hg_lb = nrm(ks[9], (DEPTH + 1, HG_FDIM), 0.1)
    hg_norm_w = 1.0 + nrm(ks[10], (DEPTH, HG_WIDTH), 0.05)
    w_out = nrm(ks[11], (DEPTH, MIX_WIDTH, D_MODEL), MIX_WIDTH ** -0.5)
    pe_norm_w = 1.0 + nrm(ks[12], (DEPTH, D_MODEL), 0.05)
    w_pg = nrm(ks[13], (DEPTH, D_MODEL, D_MODEL), D_MODEL ** -0.5)
    w_pe = nrm(ks[14], (DEPTH, PE_DIM, D_MODEL), PE_DIM ** -0.5)
    final_norm_w = 1.0 + nrm(ks[15], (D_MODEL,), 0.05)
    return {"x": x, "p": p, "norm_w": norm_w, "w_in": w_in, "conv_w": conv_w,
            "conv_b": conv_b, "ml_b_i": ml_b_i, "ml_b_f": ml_b_f, "ml_norm_w": ml_norm_w,
            "hg_lb": hg_lb, "hg_norm_w": hg_norm_w, "w_out": w_out, "pe_norm_w": pe_norm_w,
            "w_pg": w_pg, "w_pe": w_pe, "final_norm_w": final_norm_w}


def reference(x, p, norm_w, w_in, conv_w, conv_b, ml_b_i, ml_b_f, ml_norm_w,
              hg_lb, hg_norm_w, w_out, pe_norm_w, w_pg, w_pe, final_norm_w):
    Bn, S, _ = x.shape
    split_points = [int(s) for s in np.cumsum(SPLIT_SIZES)[:-1]]
    lb_all = jnp.cumsum(jax.nn.softmax(hg_lb.astype(jnp.float32), axis=0), axis=0)
    h = x.astype(jnp.float32)
    for l in range(DEPTH):
        u = rmsnorm(h, norm_w[l])
        proj = jnp.einsum('bsd,dc->bsc', u, w_in[l].astype(jnp.float32))
        (ml_q, ml_k, ml_v, ml_o, ml_z, ml_i, ml_f,
         hg_q, hg_f, hg_i, hg_g) = jnp.split(proj, split_points, axis=-1)

        qk = jax.nn.silu(causal_dwconv(jnp.concatenate([ml_q, ml_k], axis=-1),
                                       conv_w[l].astype(jnp.float32), conv_b[l].astype(jnp.float32)))
        q_a = qk[..., :ML_QK].reshape(Bn, S, ML_HEADS, ML_DQK)
        k_a = qk[..., ML_QK:].reshape(Bn, S, ML_HEADS, ML_DQK)
        v_a = ml_v.reshape(Bn, S, ML_HEADS, ML_DV)
        i_pre = ml_i + ml_b_i[l].astype(jnp.float32)
        f_pre = ml_f + ml_b_f[l].astype(jnp.float32)
        h_a = mlstm_chunkwise(q_a, k_a, v_a, i_pre, f_pre)
        h_a = rmsnorm(h_a, ml_norm_w[l].reshape(ML_HEADS, ML_DV)).reshape(Bn, S, ML_WIDTH)
        y_a = jax.nn.sigmoid(ml_o) * h_a * jax.nn.silu(ml_z)

        lb = lb_all[l]
        f_b = lb + (1.0 - lb) * jax.nn.sigmoid(hg_f)
        logf_b = jnp.log(f_b).reshape(Bn, S, HG_HEADS, HG_EXPAND)
        k_b = ((1.0 - lb) * jax.nn.sigmoid(-hg_f)).reshape(Bn, S, HG_HEADS, HG_EXPAND)
        q_b = hg_q.reshape(Bn, S, HG_HEADS, HG_EXPAND)
        i_b = hg_i.reshape(Bn, S, HG_HEADS, HG_DV)
        h_b = hgrn2_chunkwise(q_b, k_b, logf_b, i_b)
        h_b = rmsnorm(h_b, hg_norm_w[l].reshape(HG_HEADS, HG_DV)).reshape(Bn, S, HG_WIDTH)
        y_b = h_b * jax.nn.silu(hg_g)

        y = jnp.concatenate([y_a, y_b], axis=-1)
        h = h + jnp.einsum('bsc,cd->bsd', y, w_out[l].astype(jnp.float32))

        e = jnp.einsum('bsr,rd->bsd', p[l].astype(jnp.float32), w_pe[l].astype(jnp.float32))
        gate = jax.nn.sigmoid(jnp.einsum('bsd,de->bse', rmsnorm(h, pe_norm_w[l]), w_pg[l].astype(jnp.float32)))
        h = h + gate * e
    out = rmsnorm(h, final_norm_w)
    return out.astype(x.dtype)
```

```python
import functools
import math

import jax
import jax.numpy as jnp
from jax import lax
from jax.experimental import pallas as pl
from jax.experimental.pallas import tpu as pltpu

F32 = jnp.float32
BF16 = jnp.bfloat16

D_MODEL = 2048
PE_DIM = 256
ML_HEADS = 4
ML_DQK = 128
ML_DV = 256
ML_QK = ML_HEADS * ML_DQK
ML_WIDTH = ML_HEADS * ML_DV
ML_CONV = 4
HG_HEADS = 8
HG_E = 128
HG_DV = 128
HG_WIDTH = HG_HEADS * HG_DV
EPS = 1e-6

MAIN_COLS = 8192
COL_QK = 0
COL_V = 1024
COL_O = 2048
COL_Z = 3072
COL_HQ = 4096
COL_HF = 5120
COL_HI = 6144
COL_HG = 7168
GATE_COLS = 128

LANES = 128
SUBLANES = 8
NEG_BIG = -1e30
V7X_VMEM_LIMIT = 56 * 1024 * 1024

CHUNK = 128
TM_IN = 1024
TN_IN = 512
TM_OUT = 256


def _dot(a, b):
    return jnp.dot(a, b, preferred_element_type=F32)


def _dot_nt(a, b):
    return lax.dot_general(a, b, (((1,), (1,)), ((), ())), preferred_element_type=F32)


def _dot_tn(a, b):
    return lax.dot_general(a, b, (((0,), (0,)), ((), ())), preferred_element_type=F32)


def _inproj_kernel(x_ref, nw_ref, w_ref, wg_ref, proj_ref, gate_ref, u_ref):
    @pl.when(pl.program_id(1) == 0)
    def _():
        x = x_ref[...]
        ms = jnp.mean(x * x, axis=-1, keepdims=True)
        u = (x * lax.rsqrt(ms + EPS) * nw_ref[...]).astype(BF16)
        u_ref[...] = u
        gate_ref[...] = _dot(u, wg_ref[...])

    proj_ref[...] = _dot(u_ref[...], w_ref[...])


def _inproj(x2, norm_w, w_main, w_gate):
    t = x2.shape[0]
    tm = min(TM_IN, t)
    grid = (t // tm, MAIN_COLS // TN_IN)
    return pl.pallas_call(
        _inproj_kernel,
        grid=grid,
        in_specs=[
            pl.BlockSpec((tm, D_MODEL), lambda i, j: (i, 0)),
            pl.BlockSpec((1, D_MODEL), lambda i, j: (0, 0)),
            pl.BlockSpec((D_MODEL, TN_IN), lambda i, j: (0, j)),
            pl.BlockSpec((D_MODEL, GATE_COLS), lambda i, j: (0, 0)),
        ],
        out_specs=[
            pl.BlockSpec((tm, TN_IN), lambda i, j: (i, j)),
            pl.BlockSpec((tm, GATE_COLS), lambda i, j: (i, 0)),
        ],
        out_shape=[
            jax.ShapeDtypeStruct((t, MAIN_COLS), F32),
            jax.ShapeDtypeStruct((t, GATE_COLS), F32),
        ],
        scratch_shapes=[pltpu.VMEM((tm, D_MODEL), BF16)],
        compiler_params=pltpu.CompilerParams(
            dimension_semantics=("arbitrary", "arbitrary"),
            vmem_limit_bytes=V7X_VMEM_LIMIT),
        name="inproj",
    )(x2, norm_w, w_main, w_gate)


def _scan_lanes(x, op, fill):
    n = x.shape[1]
    lane = lax.broadcasted_iota(jnp.int32, x.shape, 1)
    s = 1
    while s < n:
        shifted = pltpu.roll(x, s, axis=1)
        x = op(x, jnp.where(lane >= s, shifted, fill))
        s *= 2
    return x


def _block_last(c, m, row):
    n = c.shape[0]
    if m == 1:
        return c
    if m >= SUBLANES:
        c3 = c.reshape(n // m, m, LANES)
        last = c3[:, m - 1:m, :]
        return jnp.broadcast_to(last, (n // m, m, LANES)).reshape(n, LANES)
    d = (m - 1) - (row & (m - 1))
    out = c
    s = m // 2
    while s >= 1:
        out = jnp.where((d & s) != 0, pltpu.roll(out, n - s, axis=0), out)
        s //= 2
    return out


def _sigmoid_pair(x):
    e = jnp.exp(-jnp.abs(x))
    r = 1.0 / (1.0 + e)
    er = e * r
    pos = x >= 0
    return jnp.where(pos, r, er), jnp.where(pos, er, r)


def _silu(x):
    s, _ = _sigmoid_pair(x)
    return x * s


def _mixer_kernel(proj_ref, gate_ref, convw_ref, convb_ref, gbias_ref, mlnw_ref,
                  hglb_ref, hgnw_ref, y_ref,
                  qkext_ref, c_ref, n_ref, m_ref, st_ref, lv_ref):
    L = proj_ref.shape[0]
    nlev = int(math.log2(L))
    first = pl.program_id(1) == 0

    @pl.when(jnp.logical_and(pl.program_id(0) == 0, first))
    def _():
        r = lax.broadcasted_iota(jnp.int32, (L, L), 0)
        c = lax.broadcasted_iota(jnp.int32, (L, L), 1)
        lv = 31 - lax.clz(r ^ c)
        lv_ref[...] = jnp.where(r > c, lv, -1)

    @pl.when(first)
    def _():
        qkext_ref[0:SUBLANES, :] = jnp.zeros((SUBLANES, 2 * ML_QK), F32)
        c_ref[...] = jnp.zeros(c_ref.shape, F32)
        n_ref[...] = jnp.zeros(n_ref.shape, F32)
        m_ref[...] = jnp.zeros(m_ref.shape, F32)
        st_ref[...] = jnp.zeros(st_ref.shape, F32)

    qkext_ref[SUBLANES:SUBLANES + L, :] = proj_ref[:, COL_QK:COL_QK + 2 * ML_QK]
    conv = convb_ref[...]
    for k in range(ML_CONV):
        off = SUBLANES - (ML_CONV - 1) + k
        conv = conv + qkext_ref[off:off + L, :] * convw_ref[k:k + 1, :]
    qkext_ref[0:SUBLANES, :] = qkext_ref[L:L + SUBLANES, :]
    qk = _silu(conv)
    q_all = qk[:, :ML_QK]
    k_all = qk[:, ML_QK:] * (ML_DQK ** -0.5)

    gt = (gate_ref[...] + gbias_ref[...]).T
    i8 = gt[0:SUBLANES, :]
    f8 = pltpu.roll(i8, ML_HEADS, axis=0)
    lf8 = jnp.minimum(f8, 0.0) - jnp.log1p(jnp.exp(-jnp.abs(f8)))
    b8 = _scan_lanes(lf8, jnp.add, 0.0)
    a8 = i8 - b8
    cm8 = _scan_lanes(a8, jnp.maximum, NEG_BIG)
    mprev8 = m_ref[:, 0:1]
    mm8 = jnp.maximum(mprev8, cm8)
    mlast8 = jnp.max(mm8, axis=1, keepdims=True)
    g8 = jnp.sum(lf8, axis=1, keepdims=True)
    m_ref[...] = jnp.broadcast_to(g8 + mlast8, m_ref.shape)

    table = jnp.concatenate(
        [mm8, b8, a8,
         jnp.broadcast_to(mlast8, (SUBLANES, L)),
         jnp.broadcast_to(mprev8, (SUBLANES, L)),
         jnp.zeros((LANES - 5 * SUBLANES, L), F32)], axis=0)
    ct = table.T

    rowi = lax.broadcasted_iota(jnp.int32, (L, L), 0)
    coli = lax.broadcasted_iota(jnp.int32, (L, L), 1)
    causal = coli <= rowi

    for h in range(ML_HEADS):
        mm_c = ct[:, h:h + 1]
        b_c = ct[:, 8 + h:9 + h]
        a_c = ct[:, 16 + h:17 + h]
        mlast_c = ct[:, 24 + h:25 + h]
        mprev_c = ct[:, 32 + h:33 + h]
        a_r = a8[h:h + 1, :]

        qh = q_all[:, h * ML_DQK:(h + 1) * ML_DQK]
        kh = k_all[:, h * ML_DQK:(h + 1) * ML_DQK]
        qb = qh.astype(BF16)
        vb = proj_ref[:, COL_V + h * ML_DV:COL_V + (h + 1) * ML_DV].astype(BF16)

        decay = jnp.exp(jnp.where(causal, a_r - mm_c, NEG_BIG))
        scores = _dot_nt(qb, kh.astype(BF16)) * decay
        w_inter = jnp.exp(mprev_c - mm_c)
        c_old = c_ref[h]
        n_old = n_ref[h]
        num = _dot(scores.astype(BF16), vb) + w_inter * _dot(qb, c_old.astype(BF16))
        den = (jnp.sum(scores, axis=-1, keepdims=True)
               + w_inter * jnp.sum(qh * n_old[0:1, :], axis=-1, keepdims=True))
        hh = num / jnp.maximum(jnp.abs(den), jnp.exp(-(b_c + mm_c)))

        wa_c = jnp.exp(a_c - mlast_c)
        ws_c = jnp.exp(mprev_c - mlast_c)
        kw = kh * wa_c
        c_ref[h] = ws_c[0:ML_DQK, :] * c_old + _dot_tn(kw.astype(BF16), vb)
        n_ref[h] = ws_c[0:SUBLANES, :] * n_old + jnp.sum(kw, axis=0, keepdims=True)

        hn = hh * lax.rsqrt(jnp.mean(hh * hh, axis=-1, keepdims=True) + EPS)
        hn = hn * mlnw_ref[:, h * ML_DV:(h + 1) * ML_DV]
        o_gate, _ = _sigmoid_pair(proj_ref[:, COL_O + h * ML_DV:COL_O + (h + 1) * ML_DV])
        z = proj_ref[:, COL_Z + h * ML_DV:COL_Z + (h + 1) * ML_DV]
        y_ref[:, h * ML_DV:(h + 1) * ML_DV] = (o_gate * hn * _silu(z)).astype(y_ref.dtype)

    row = lax.broadcasted_iota(jnp.int32, (L, LANES), 0)
    lbl = hglb_ref[...]
    lmax = jnp.max(lbl, axis=0, keepdims=True)
    lexp = jnp.exp(lbl - lmax)
    lb_all = lexp[0:1, :] / jnp.sum(lexp, axis=0, keepdims=True)

    for h in range(HG_HEADS):
        sl = slice(h * HG_E, (h + 1) * HG_E)
        lb = lb_all[:, sl]
        q = proj_ref[:, COL_HQ + h * HG_E:COL_HQ + (h + 1) * HG_E]
        v = proj_ref[:, COL_HI + h * HG_DV:COL_HI + (h + 1) * HG_DV]
        s_pos, s_neg = _sigmoid_pair(proj_ref[:, COL_HF + h * HG_E:COL_HF + (h + 1) * HG_E])
        kk = (1.0 - lb) * s_neg
        c = jnp.log(lb + (1.0 - lb) * s_pos)

        att = None
        for lev in range(nlev):
            m = 1 << lev
            upper = ((row >> lev) & 1) == 1
            tot = _block_last(c, m, row)
            e = jnp.exp(jnp.where(upper, c, tot - c))
            zz = (jnp.where(upper, q, kk) * e).astype(BF16)
            p = jnp.where(lv_ref[...] == lev, _dot_nt(zz, zz), 0.0)
            att = p if att is None else att + p
            c = c + jnp.where(upper, pltpu.roll(tot, m, axis=0), 0.0)

        tot = _block_last(c, L, row)
        st_old = st_ref[h]
        vb = v.astype(BF16)
        o = (_dot(att.astype(BF16), vb)
             + _dot_nt((q * jnp.exp(c)).astype(BF16), st_old.astype(BF16))
             + jnp.sum(q * kk, axis=-1, keepdims=True) * v)
        ke = (kk * jnp.exp(tot - c)).astype(BF16)
        st_ref[h] = st_old * jnp.exp(tot[0:HG_DV, :]) + _dot_tn(vb, ke)

        on = o * lax.rsqrt(jnp.mean(o * o, axis=-1, keepdims=True) + EPS) * hgnw_ref[:, sl]
        gz = proj_ref[:, COL_HG + h * HG_DV:COL_HG + (h + 1) * HG_DV]
        y_ref[:, ML_WIDTH + h * HG_DV:ML_WIDTH + (h + 1) * HG_DV] = (on * _silu(gz)).astype(y_ref.dtype)


def _mixers(proj, gates, conv_w, conv_b, gbias, ml_norm_w, hg_lb, hg_norm_w, batch, seq):
    L = min(CHUNK, seq)
    nc = seq // L
    full = lambda shape: pl.BlockSpec(shape, lambda b, c: (0, 0))
    return pl.pallas_call(
        _mixer_kernel,
        grid=(batch, nc),
        in_specs=[
            pl.BlockSpec((L, MAIN_COLS), lambda b, c: (b * nc + c, 0)),
            pl.BlockSpec((L, GATE_COLS), lambda b, c: (b * nc + c, 0)),
            full((ML_CONV, 2 * ML_QK)),
            full((1, 2 * ML_QK)),
            full((1, GATE_COLS)),
            full((1, ML_WIDTH)),
            full((2, HG_WIDTH)),
            full((1, HG_WIDTH)),
        ],
        out_specs=pl.BlockSpec((L, ML_WIDTH + HG_WIDTH), lambda b, c: (b * nc + c, 0)),
        out_shape=jax.ShapeDtypeStruct((batch * seq, ML_WIDTH + HG_WIDTH), BF16),
        scratch_shapes=[
            pltpu.VMEM((L + SUBLANES, 2 * ML_QK), F32),
            pltpu.VMEM((ML_HEADS, ML_DQK, ML_DV), F32),
            pltpu.VMEM((ML_HEADS, SUBLANES, ML_DQK), F32),
            pltpu.VMEM((SUBLANES, LANES), F32),
            pltpu.VMEM((HG_HEADS, HG_DV, HG_E), F32),
            pltpu.VMEM((L, L), jnp.int32),
        ],
        compiler_params=pltpu.CompilerParams(
            dimension_semantics=("arbitrary", "arbitrary"),
            vmem_limit_bytes=V7X_VMEM_LIMIT),
        name="mixers",
    )(proj, gates, conv_w, conv_b, gbias, ml_norm_w, hg_lb, hg_norm_w)


def _out_kernel(x_ref, y_ref, p_ref, wout_ref, wpg_ref, wpe_ref, pnw_ref, fnw_ref, o_ref):
    h = x_ref[...] + _dot(y_ref[...], wout_ref[...])
    hn = h * lax.rsqrt(jnp.mean(h * h, axis=-1, keepdims=True) + EPS) * pnw_ref[...]
    gate, _ = _sigmoid_pair(_dot(hn.astype(BF16), wpg_ref[...]))
    emb = _dot(p_ref[...].astype(BF16), wpe_ref[...])
    h = h + gate * emb
    o_ref[...] = h * lax.rsqrt(jnp.mean(h * h, axis=-1, keepdims=True) + EPS) * fnw_ref[...]


def _out_stage(x2, y, p2, w_out, w_pg, w_pe, pe_norm_w, final_norm_w):
    t = x2.shape[0]
    tm = min(TM_OUT, t)
    const = lambda shape: pl.BlockSpec(shape, lambda i: (0, 0), pipeline_mode=pl.Buffered(1))
    return pl.pallas_call(
        _out_kernel,
        grid=(t // tm,),
        in_specs=[
            pl.BlockSpec((tm, D_MODEL), lambda i: (i, 0)),
            pl.BlockSpec((tm, D_MODEL), lambda i: (i, 0)),
            pl.BlockSpec((tm, PE_DIM), lambda i: (i, 0)),
            const((D_MODEL, D_MODEL)),
            const((D_MODEL, D_MODEL)),
            const((PE_DIM, D_MODEL)),
            const((1, D_MODEL)),
            const((1, D_MODEL)),
        ],
        out_specs=pl.BlockSpec((tm, D_MODEL), lambda i: (i, 0)),
        out_shape=jax.ShapeDtypeStruct((t, D_MODEL), F32),
        compiler_params=pltpu.CompilerParams(
            dimension_semantics=("arbitrary",),
            vmem_limit_bytes=V7X_VMEM_LIMIT),
        name="out_stage",
    )(x2, y, p2, w_out, w_pg, w_pe, pe_norm_w, final_norm_w)


def kernel(x, p, norm_w, w_in, conv_w, conv_b, ml_b_i, ml_b_f, ml_norm_w, hg_lb, hg_norm_w,
           w_out, pe_norm_w, w_pg, w_pe, final_norm_w):
    batch, seq, _ = x.shape
    t = batch * seq
    x2 = x.reshape(t, D_MODEL).astype(F32)
    p2 = p[0].reshape(t, PE_DIM)

    w = w_in[0]
    n_gate = 2 * ML_HEADS
    gate_lo = 2 * ML_QK + 3 * ML_WIDTH
    w_main = jnp.concatenate([w[:, :gate_lo], w[:, gate_lo + n_gate:]], axis=1).astype(BF16)
    w_gate = jnp.pad(w[:, gate_lo:gate_lo + n_gate], ((0, 0), (0, GATE_COLS - n_gate))).astype(BF16)
    gbias = jnp.pad(jnp.concatenate([ml_b_i[0], ml_b_f[0]]).astype(F32),
                    (0, GATE_COLS - n_gate)).reshape(1, GATE_COLS)

    proj, gates = _inproj(x2, norm_w[0].reshape(1, D_MODEL).astype(F32), w_main, w_gate)
    y = _mixers(proj, gates, conv_w[0].astype(F32), conv_b[0].reshape(1, -1).astype(F32), gbias,
                ml_norm_w[0].reshape(1, -1).astype(F32), hg_lb.astype(F32),
                hg_norm_w[0].reshape(1, -1).astype(F32), batch, seq)
    out = _out_stage(x2, y, p2.astype(F32), w_out[0].astype(BF16), w_pg[0].astype(BF16),
                     w_pe[0].astype(BF16), pe_norm_w[0].reshape(1, -1).astype(F32),
                     final_norm_w.reshape(1, -1).astype(F32))
    return out.reshape(batch, seq, D_MODEL).astype(x.dtype)
```

```python
import functools
import math

import jax
import jax.numpy as jnp
from jax import lax
from jax.experimental import pallas as pl
from jax.experimental.pallas import tpu as pltpu

F32 = jnp.float32
BF16 = jnp.bfloat16

D_MODEL = 2048
PE_DIM = 256
ML_HEADS = 4
ML_DQK = 128
ML_DV = 256
ML_QK = ML_HEADS * ML_DQK
ML_WIDTH = ML_HEADS * ML_DV
ML_CONV = 4
HG_HEADS = 8
HG_E = 128
HG_DV = 128
HG_WIDTH = HG_HEADS * HG_DV
EPS = 1e-6

MAIN_COLS = 8192
COL_QK = 0
COL_V = 1024
COL_O = 2048
COL_Z = 3072
COL_HQ = 4096
COL_HF = 5120
COL_HI = 6144
COL_HG = 7168
GATE_COLS = 128

LANES = 128
SUBLANES = 8
NEG_BIG = -1e30
V7X_VMEM_LIMIT = 58 * 1024 * 1024

CHUNK = 128
TN_IN = 256
TN_OUT = 512
TM_OUT = 256


def _dot(a, b):
    return jnp.dot(a, b, preferred_element_type=F32)


def _dot_nt(a, b):
    return lax.dot_general(a, b, (((1,), (1,)), ((), ())), preferred_element_type=F32)


def _dot_tn(a, b):
    return lax.dot_general(a, b, (((0,), (0,)), ((), ())), preferred_element_type=F32)


def _col_tiles(w, tn):
    k, n = w.shape
    return w.reshape(k, n // tn, tn).transpose(1, 0, 2)


def _dot_tiled(a, w_ref):
    return jnp.concatenate([_dot(a, w_ref[j]) for j in range(w_ref.shape[0])], axis=1)


def _rms_scale(x):
    return lax.rsqrt(jnp.mean(x * x, axis=-1, keepdims=True) + EPS)


def _scan_lanes(x, op, fill):
    n = x.shape[1]
    lane = lax.broadcasted_iota(jnp.int32, x.shape, 1)
    s = 1
    while s < n:
        shifted = pltpu.roll(x, s, axis=1)
        x = op(x, jnp.where(lane >= s, shifted, fill))
        s *= 2
    return x


def _block_last(c, m, row):
    n = c.shape[0]
    if m == 1:
        return c
    if m >= SUBLANES:
        c3 = c.reshape(n // m, m, LANES)
        last = c3[:, m - 1:m, :]
        return jnp.broadcast_to(last, (n // m, m, LANES)).reshape(n, LANES)
    d = (m - 1) - (row & (m - 1))
    out = c
    s = m // 2
    while s >= 1:
        out = jnp.where((d & s) != 0, pltpu.roll(out, n - s, axis=0), out)
        s //= 2
    return out


def _sigmoid_pair(x):
    e = jnp.exp(-jnp.abs(x))
    r = 1.0 / (1.0 + e)
    er = e * r
    pos = x >= 0
    return jnp.where(pos, r, er), jnp.where(pos, er, r)


def _silu(x):
    s, _ = _sigmoid_pair(x)
    return x * s


def _projection_tasks(x_ref, nw_ref, w_ref, wg_ref, u_ref, proj_dst, gate_dst):
    n_tiles, _, tn = w_ref.shape

    def head():
        x = x_ref[...]
        u_ref[...] = (x * _rms_scale(x) * nw_ref[...]).astype(BF16)
        gate_dst[...] = _dot(u_ref[...], wg_ref[...])

    def tile(j):
        proj_dst[:, j * tn:(j + 1) * tn] = _dot(u_ref[...], w_ref[j])

    return [head] + [functools.partial(tile, j) for j in range(n_tiles)]


def _project(*args):
    for task in _projection_tasks(*args):
        task()


PROJ_TILES_PER_STAGE = (4, 2, 1, 2, 1, 3, 3, 3, 3, 3, 3, 2, 2)


def _mix(proj_ref, gate_ref, convw_ref, convb_ref, gbias_ref, mlnw_ref, hglb_ref, hgnw_ref,
         y_ref, r0, qkext_ref, c_ref, n_ref, m_ref, st_ref, lv_ref, tasks):
    L = proj_ref.shape[0]
    nlev = int(math.log2(L))
    rows = slice(r0, r0 + L)
    tasks = list(tasks)
    tasks.pop(0)()
    stage_tiles = iter(PROJ_TILES_PER_STAGE)

    def fill():
        for _ in range(next(stage_tiles)):
            tasks.pop(0)()

    qkext_ref[SUBLANES:SUBLANES + L, :] = proj_ref[:, COL_QK:COL_QK + 2 * ML_QK]
    conv = convb_ref[...]
    for k in range(ML_CONV):
        off = SUBLANES - (ML_CONV - 1) + k
        conv = conv + qkext_ref[off:off + L, :] * convw_ref[k:k + 1, :]
    qkext_ref[0:SUBLANES, :] = qkext_ref[L:L + SUBLANES, :]
    qk = _silu(conv)
    q_all = qk[:, :ML_QK]
    k_all = qk[:, ML_QK:] * (ML_DQK ** -0.5)

    gt = (gate_ref[...] + gbias_ref[...]).T
    i8 = gt[0:SUBLANES, :]
    f8 = pltpu.roll(i8, ML_HEADS, axis=0)
    lf8 = jnp.minimum(f8, 0.0) - jnp.log1p(jnp.exp(-jnp.abs(f8)))
    b8 = _scan_lanes(lf8, jnp.add, 0.0)
    a8 = i8 - b8
    cm8 = _scan_lanes(a8, jnp.maximum, NEG_BIG)
    mprev8 = m_ref[:, 0:1]
    mm8 = jnp.maximum(mprev8, cm8)
    mlast8 = jnp.max(mm8, axis=1, keepdims=True)
    g8 = jnp.sum(lf8, axis=1, keepdims=True)
    m_ref[...] = jnp.broadcast_to(g8 + mlast8, m_ref.shape)

    table = jnp.concatenate(
        [mm8, b8, a8,
         jnp.broadcast_to(mlast8, (SUBLANES, L)),
         jnp.broadcast_to(mprev8, (SUBLANES, L)),
         jnp.zeros((LANES - 5 * SUBLANES, L), F32)], axis=0)
    ct = table.T

    rowi = lax.broadcasted_iota(jnp.int32, (L, L), 0)
    coli = lax.broadcasted_iota(jnp.int32, (L, L), 1)
    causal = coli <= rowi
    fill()

    for h in range(ML_HEADS):
        mm_c = ct[:, h:h + 1]
        b_c = ct[:, 8 + h:9 + h]
        a_c = ct[:, 16 + h:17 + h]
        mlast_c = ct[:, 24 + h:25 + h]
        mprev_c = ct[:, 32 + h:33 + h]
        a_r = a8[h:h + 1, :]

        qh = q_all[:, h * ML_DQK:(h + 1) * ML_DQK]
        kh = k_all[:, h * ML_DQK:(h + 1) * ML_DQK]
        qb = qh.astype(BF16)
        vb = proj_ref[:, COL_V + h * ML_DV:COL_V + (h + 1) * ML_DV].astype(BF16)

        decay = jnp.exp(jnp.where(causal, a_r - mm_c, NEG_BIG))
        scores = _dot_nt(qb, kh.astype(BF16)) * decay
        w_inter = jnp.exp(mprev_c - mm_c)
        c_old = c_ref[h]
        n_old = n_ref[h]
        num = _dot(scores.astype(BF16), vb) + w_inter * _dot(qb, c_old.astype(BF16))
        den = (jnp.sum(scores, axis=-1, keepdims=True)
               + w_inter * jnp.sum(qh * n_old[0:1, :], axis=-1, keepdims=True))
        hh = num / jnp.maximum(jnp.abs(den), jnp.exp(-(b_c + mm_c)))

        wa_c = jnp.exp(a_c - mlast_c)
        ws_c = jnp.exp(mprev_c - mlast_c)
        kw = kh * wa_c
        c_ref[h] = ws_c[0:ML_DQK, :] * c_old + _dot_tn(kw.astype(BF16), vb)
        n_ref[h] = ws_c[0:SUBLANES, :] * n_old + jnp.sum(kw, axis=0, keepdims=True)

        hn = hh * _rms_scale(hh) * mlnw_ref[:, h * ML_DV:(h + 1) * ML_DV]
        o_gate, _ = _sigmoid_pair(proj_ref[:, COL_O + h * ML_DV:COL_O + (h + 1) * ML_DV])
        z = proj_ref[:, COL_Z + h * ML_DV:COL_Z + (h + 1) * ML_DV]
        y_ref[rows, h * ML_DV:(h + 1) * ML_DV] = (o_gate * hn * _silu(z)).astype(y_ref.dtype)
        fill()

    row = lax.broadcasted_iota(jnp.int32, (L, LANES), 0)
    lbl = hglb_ref[...]
    lmax = jnp.max(lbl, axis=0, keepdims=True)
    lexp = jnp.exp(lbl - lmax)
    lb_all = lexp[0:1, :] / jnp.sum(lexp, axis=0, keepdims=True)

    for h in range(HG_HEADS):
        sl = slice(h * HG_E, (h + 1) * HG_E)
        lb = lb_all[:, sl]
        q = proj_ref[:, COL_HQ + h * HG_E:COL_HQ + (h + 1) * HG_E]
        v = proj_ref[:, COL_HI + h * HG_DV:COL_HI + (h + 1) * HG_DV]
        s_pos, s_neg = _sigmoid_pair(proj_ref[:, COL_HF + h * HG_E:COL_HF + (h + 1) * HG_E])
        kk = (1.0 - lb) * s_neg
        c = jnp.log(lb + (1.0 - lb) * s_pos)

        att = None
        for lev in range(nlev):
            m = 1 << lev
            upper = ((row >> lev) & 1) == 1
            tot = _block_last(c, m, row)
            e = jnp.exp(jnp.where(upper, c, tot - c))
            zz = (jnp.where(upper, q, kk) * e).astype(BF16)
            p = jnp.where(lv_ref[...] == lev, _dot_nt(zz, zz), 0.0)
            att = p if att is None else att + p
            c = c + jnp.where(upper, pltpu.roll(tot, m, axis=0), 0.0)

        tot = _block_last(c, L, row)
        st_old = st_ref[h]
        vb = v.astype(BF16)
        o = (_dot(att.astype(BF16), vb)
             + _dot_nt((q * jnp.exp(c)).astype(BF16), st_old.astype(BF16))
             + jnp.sum(q * kk, axis=-1, keepdims=True) * v)
        ke = (kk * jnp.exp(tot - c)).astype(BF16)
        st_ref[h] = st_old * jnp.exp(tot[0:HG_DV, :]) + _dot_tn(vb, ke)

        on = o * _rms_scale(o) * hgnw_ref[:, sl]
        gz = proj_ref[:, COL_HG + h * HG_DV:COL_HG + (h + 1) * HG_DV]
        y_ref[rows, ML_WIDTH + h * HG_DV:ML_WIDTH + (h + 1) * HG_DV] = (on * _silu(gz)).astype(y_ref.dtype)
        fill()
    assert not tasks


def _mixer_kernel(steps_per_seq,
                  x0_ref, xa_ref, xb_ref, nw_ref, w_ref, wg_ref,
                  convw_ref, convb_ref, gbias_ref, mlnw_ref, hglb_ref, hgnw_ref,
                  y_ref,
                  proj_a, proj_b, gate_a, gate_b, u_a, u_b,
                  qkext_ref, c_ref, n_ref, m_ref, st_ref, lv_ref):
    L = proj_a.shape[0]
    g = pl.program_id(0)

    @pl.when(g == 0)
    def _():
        _project(x0_ref, nw_ref, w_ref, wg_ref, u_a, proj_a, gate_a)
        r = lax.broadcasted_iota(jnp.int32, (L, L), 0)
        c = lax.broadcasted_iota(jnp.int32, (L, L), 1)
        lv = 31 - lax.clz(r ^ c)
        lv_ref[...] = jnp.where(r > c, lv, -1)

    @pl.when(g % steps_per_seq == 0)
    def _():
        qkext_ref[0:SUBLANES, :] = jnp.zeros((SUBLANES, 2 * ML_QK), F32)
        c_ref[...] = jnp.zeros(c_ref.shape, F32)
        n_ref[...] = jnp.zeros(n_ref.shape, F32)
        m_ref[...] = jnp.zeros(m_ref.shape, F32)
        st_ref[...] = jnp.zeros(st_ref.shape, F32)

    params = (convw_ref, convb_ref, gbias_ref, mlnw_ref, hglb_ref, hgnw_ref)
    state = (qkext_ref, c_ref, n_ref, m_ref, st_ref, lv_ref)
    _mix(proj_a, gate_a, *params, y_ref, 0, *state,
         _projection_tasks(xa_ref, nw_ref, w_ref, wg_ref, u_b, proj_b, gate_b))
    _mix(proj_b, gate_b, *params, y_ref, L, *state,
         _projection_tasks(xb_ref, nw_ref, w_ref, wg_ref, u_a, proj_a, gate_a))


def _mixers(x2, norm_w, w_main, w_gate, conv_w, conv_b, gbias, ml_norm_w, hg_lb,
            hg_norm_w, chunk, chunks_per_seq):
    t = x2.shape[0]
    L = chunk
    nchunks = t // L
    steps = nchunks // 2
    assert chunks_per_seq % 2 == 0
    const = lambda shape: pl.BlockSpec(shape, lambda g: (0, 0), pipeline_mode=pl.Buffered(1))
    return pl.pallas_call(
        functools.partial(_mixer_kernel, chunks_per_seq // 2),
        grid=(steps,),
        in_specs=[
            const((L, D_MODEL)),
            pl.BlockSpec((L, D_MODEL), lambda g: (2 * g + 1, 0)),
            pl.BlockSpec((L, D_MODEL), lambda g: (jnp.minimum(2 * g + 2, nchunks - 1), 0)),
            const((1, D_MODEL)),
            pl.BlockSpec(w_main.shape, lambda g: (0, 0, 0), pipeline_mode=pl.Buffered(1)),
            const((D_MODEL, GATE_COLS)),
            const((ML_CONV, 2 * ML_QK)),
            const((1, 2 * ML_QK)),
            const((1, GATE_COLS)),
            const((1, ML_WIDTH)),
            const((2, HG_WIDTH)),
            const((1, HG_WIDTH)),
        ],
        out_specs=pl.BlockSpec((2 * L, ML_WIDTH + HG_WIDTH), lambda g: (g, 0)),
        out_shape=jax.ShapeDtypeStruct((t, ML_WIDTH + HG_WIDTH), BF16),
        scratch_shapes=[
            pltpu.VMEM((L, MAIN_COLS), F32),
            pltpu.VMEM((L, MAIN_COLS), F32),
            pltpu.VMEM((L, GATE_COLS), F32),
            pltpu.VMEM((L, GATE_COLS), F32),
            pltpu.VMEM((L, D_MODEL), BF16),
            pltpu.VMEM((L, D_MODEL), BF16),
            pltpu.VMEM((L + SUBLANES, 2 * ML_QK), F32),
            pltpu.VMEM((ML_HEADS, ML_DQK, ML_DV), F32),
            pltpu.VMEM((ML_HEADS, SUBLANES, ML_DQK), F32),
            pltpu.VMEM((SUBLANES, LANES), F32),
            pltpu.VMEM((HG_HEADS, HG_DV, HG_E), F32),
            pltpu.VMEM((L, L), jnp.int32),
        ],
        compiler_params=pltpu.CompilerParams(
            dimension_semantics=("arbitrary",),
            vmem_limit_bytes=V7X_VMEM_LIMIT),
        name="mixers",
    )(x2, x2, x2, norm_w, w_main, w_gate, conv_w, conv_b, gbias, ml_norm_w, hg_lb, hg_norm_w)


def _out_kernel(x_ref, y_ref, p_ref, wout_ref, wpg_ref, wpe_ref, pnw_ref, fnw_ref, o_ref):
    h = x_ref[...] + _dot_tiled(y_ref[...], wout_ref)
    hn = h * _rms_scale(h) * pnw_ref[...]
    gate, _ = _sigmoid_pair(_dot_tiled(hn.astype(BF16), wpg_ref))
    emb = _dot_tiled(p_ref[...].astype(BF16), wpe_ref)
    h = h + gate * emb
    o_ref[...] = h * _rms_scale(h) * fnw_ref[...]


def _out_stage(x2, y, p2, w_out, w_pg, w_pe, pe_norm_w, final_norm_w):
    t = x2.shape[0]
    tm = min(TM_OUT, t)
    const = lambda shape: pl.BlockSpec(shape, lambda i: (0, 0), pipeline_mode=pl.Buffered(1))
    const3 = lambda shape: pl.BlockSpec(shape, lambda i: (0, 0, 0), pipeline_mode=pl.Buffered(1))
    return pl.pallas_call(
        _out_kernel,
        grid=(t // tm,),
        in_specs=[
            pl.BlockSpec((tm, D_MODEL), lambda i: (i, 0)),
            pl.BlockSpec((tm, D_MODEL), lambda i: (i, 0)),
            pl.BlockSpec((tm, PE_DIM), lambda i: (i, 0)),
            const3(w_out.shape),
            const3(w_pg.shape),
            const3(w_pe.shape),
            const((1, D_MODEL)),
            const((1, D_MODEL)),
        ],
        out_specs=pl.BlockSpec((tm, D_MODEL), lambda i: (i, 0)),
        out_shape=jax.ShapeDtypeStruct((t, D_MODEL), F32),
        compiler_params=pltpu.CompilerParams(
            dimension_semantics=("arbitrary",),
            vmem_limit_bytes=V7X_VMEM_LIMIT),
        name="out_stage",
    )(x2, y, p2, w_out, w_pg, w_pe, pe_norm_w, final_norm_w)


def kernel(x, p, norm_w, w_in, conv_w, conv_b, ml_b_i, ml_b_f, ml_norm_w, hg_lb, hg_norm_w,
           w_out, pe_norm_w, w_pg, w_pe, final_norm_w):
    batch, seq, _ = x.shape
    t = batch * seq
    chunk = min(CHUNK, seq)
    x2 = x.reshape(t, D_MODEL).astype(F32)
    p2 = p[0].reshape(t, PE_DIM)

    w = w_in[0]
    n_gate = 2 * ML_HEADS
    gate_lo = 2 * ML_QK + 3 * ML_WIDTH
    w_main = jnp.concatenate([w[:, :gate_lo], w[:, gate_lo + n_gate:]], axis=1).astype(BF16)
    w_main = _col_tiles(w_main, TN_IN)
    w_gate = jnp.pad(w[:, gate_lo:gate_lo + n_gate], ((0, 0), (0, GATE_COLS - n_gate))).astype(BF16)
    gbias = jnp.pad(jnp.concatenate([ml_b_i[0], ml_b_f[0]]).astype(F32),
                    (0, GATE_COLS - n_gate)).reshape(1, GATE_COLS)
    nw = norm_w[0].reshape(1, D_MODEL).astype(F32)

    y = _mixers(x2, nw, w_main, w_gate,
                conv_w[0].astype(F32), conv_b[0].reshape(1, -1).astype(F32), gbias,
                ml_norm_w[0].reshape(1, -1).astype(F32), hg_lb.astype(F32),
                hg_norm_w[0].reshape(1, -1).astype(F32), chunk, seq // chunk)
    out = _out_stage(x2, y, p2.astype(F32), _col_tiles(w_out[0].astype(BF16), TN_OUT),
                     _col_tiles(w_pg[0].astype(BF16), TN_OUT),
                     _col_tiles(w_pe[0].astype(BF16), TN_OUT), pe_norm_w[0].reshape(1, -1).astype(F32),
                     final_norm_w.reshape(1, -1).astype(F32))
    return out.reshape(batch, seq, D_MODEL).astype(x.dtype)
```

```python
import functools
import math

import jax
import jax.numpy as jnp
from jax import lax
from jax.experimental import pallas as pl
from jax.experimental.pallas import tpu as pltpu

F32 = jnp.float32
BF16 = jnp.bfloat16

D_MODEL = 2048
PE_DIM = 256
ML_HEADS = 4
ML_DQK = 128
ML_DV = 256
ML_QK = ML_HEADS * ML_DQK
ML_WIDTH = ML_HEADS * ML_DV
ML_CONV = 4
HG_HEADS = 8
HG_E = 128
HG_DV = 128
HG_WIDTH = HG_HEADS * HG_DV
EPS = 1e-6

MAIN_COLS = 8192
COL_QK = 0
COL_V = 1024
COL_O = 2048
COL_Z = 3072
COL_HQ = 4096
COL_HF = 5120
COL_HI = 6144
COL_HG = 7168
GATE_ROWS = 16

LANES = 128
SUBLANES = 8
NEG_BIG = -1e30
V7X_VMEM_LIMIT = 58 * 1024 * 1024

CHUNK = 128
TN_IN = 256
TN_OUT = 512
TM_OUT = 256


def _dot(a, b):
    return jnp.dot(a, b, preferred_element_type=F32)


def _dot_nt(a, b):
    return lax.dot_general(a, b, (((1,), (1,)), ((), ())), preferred_element_type=F32)


def _dot_tn(a, b):
    return lax.dot_general(a, b, (((0,), (0,)), ((), ())), preferred_element_type=F32)


def _col_tiles(w, tn):
    k, n = w.shape
    return w.reshape(k, n // tn, tn).transpose(1, 0, 2)


def _dot_tiled(a, w_ref):
    return jnp.concatenate([_dot(a, w_ref[j]) for j in range(w_ref.shape[0])], axis=1)


def _rms_scale(x):
    return lax.rsqrt(jnp.mean(x * x, axis=-1, keepdims=True) + EPS)


def _scan_lanes(x, op, fill):
    n = x.shape[1]
    lane = lax.broadcasted_iota(jnp.int32, x.shape, 1)
    s = 1
    while s < n:
        shifted = pltpu.roll(x, s, axis=1)
        x = op(x, jnp.where(lane >= s, shifted, fill))
        s *= 2
    return x


def _block_last(c, m, row):
    n = c.shape[0]
    if m == 1:
        return c
    if m >= SUBLANES:
        w = c.shape[1]
        c3 = c.reshape(n // m, m, w)
        last = c3[:, m - 1:m, :]
        return jnp.broadcast_to(last, (n // m, m, w)).reshape(n, w)
    d = (m - 1) - (row & (m - 1))
    out = c
    s = m // 2
    while s >= 1:
        out = jnp.where((d & s) != 0, pltpu.roll(out, n - s, axis=0), out)
        s //= 2
    return out


def _sigmoid_pair(x):
    e = jnp.exp(-jnp.abs(x))
    r = 1.0 / (1.0 + e)
    er = e * r
    pos = x >= 0
    return jnp.where(pos, r, er), jnp.where(pos, er, r)


def _silu(x):
    s, _ = _sigmoid_pair(x)
    return x * s


def _normalise(x, nw_ref, u_ref):
    u_ref[...] = (x * _rms_scale(x) * nw_ref[...]).astype(BF16)


def _projection_tasks(u_ref, w_ref, wg_ref, proj_dst, gate_dst):
    n_tiles, _, tn = w_ref.shape

    def gates():
        gate_dst[...] = _dot_nt(wg_ref[...], u_ref[...])

    def tile(j):
        proj_dst[:, j * tn:(j + 1) * tn] = _dot(u_ref[...], w_ref[j])

    return [gates] + [functools.partial(tile, j) for j in range(n_tiles)]


class _Filler:
    def __init__(self, tasks):
        self._tasks = list(tasks)

    def __call__(self, n=1):
        for _ in range(n):
            if self._tasks:
                self._tasks.pop(0)()

    def flush(self):
        self(len(self._tasks))


def _mlstm_gates(gate_ref, gbias_ref, m_ref, L):
    i8 = gate_ref[0:SUBLANES, :] + gbias_ref[0:SUBLANES, 0:1]
    f8 = pltpu.roll(i8, ML_HEADS, axis=0)
    lf8 = jnp.minimum(f8, 0.0) - jnp.log1p(jnp.exp(-jnp.abs(f8)))
    b8 = _scan_lanes(lf8, jnp.add, 0.0)
    a8 = i8 - b8
    cm8 = _scan_lanes(a8, jnp.maximum, NEG_BIG)
    mprev8 = m_ref[:, 0:1]
    mm8 = jnp.maximum(mprev8, cm8)
    mlast8 = jnp.max(mm8, axis=1, keepdims=True)
    g8 = jnp.sum(lf8, axis=1, keepdims=True)
    m_ref[...] = jnp.broadcast_to(g8 + mlast8, m_ref.shape)
    table = jnp.concatenate(
        [mm8, b8, a8,
         jnp.broadcast_to(mlast8, (SUBLANES, L)),
         jnp.broadcast_to(mprev8, (SUBLANES, L)),
         jnp.zeros((LANES - 5 * SUBLANES, L), F32)], axis=0)
    return a8, table.T


def _mlstm_head(h, q_all, k_all, qk_h, a8, ct, causal, proj_ref, mlnw_ref, c_ref, n_ref, y_ref, rows,
                fill):
    mm_c = ct[:, h:h + 1]
    b_c = ct[:, 8 + h:9 + h]
    a_c = ct[:, 16 + h:17 + h]
    mlast_c = ct[:, 24 + h:25 + h]
    mprev_c = ct[:, 32 + h:33 + h]
    a_r = a8[h:h + 1, :]

    qh = q_all[:, h * ML_DQK:(h + 1) * ML_DQK]
    kh = k_all[:, h * ML_DQK:(h + 1) * ML_DQK]
    qb = qh.astype(BF16)
    vb = proj_ref[:, COL_V + h * ML_DV:COL_V + (h + 1) * ML_DV].astype(BF16)

    decay = jnp.exp(jnp.where(causal, a_r - mm_c, NEG_BIG))
    scores = qk_h * decay
    w_inter = jnp.exp(mprev_c - mm_c)
    c_old = c_ref[h]
    n_old = n_ref[h]
    num = _dot(scores.astype(BF16), vb) + w_inter * _dot(qb, c_old.astype(BF16))
    den = (jnp.sum(scores, axis=-1, keepdims=True)
           + w_inter * jnp.sum(qh * n_old[0:1, :], axis=-1, keepdims=True))
    hh = num / jnp.maximum(jnp.abs(den), jnp.exp(-(b_c + mm_c)))
    fill()

    wa_c = jnp.exp(a_c - mlast_c)
    ws_c = jnp.exp(mprev_c - mlast_c)
    kw = kh * wa_c
    c_ref[h] = ws_c[0:ML_DQK, :] * c_old + _dot_tn(kw.astype(BF16), vb)
    n_ref[h] = ws_c[0:SUBLANES, :] * n_old + jnp.sum(kw, axis=0, keepdims=True)

    hn = hh * _rms_scale(hh) * mlnw_ref[:, h * ML_DV:(h + 1) * ML_DV]
    o_gate, _ = _sigmoid_pair(proj_ref[:, COL_O + h * ML_DV:COL_O + (h + 1) * ML_DV])
    z = proj_ref[:, COL_Z + h * ML_DV:COL_Z + (h + 1) * ML_DV]
    y_ref[rows, h * ML_DV:(h + 1) * ML_DV] = (o_gate * hn * _silu(z)).astype(y_ref.dtype)


def _split_halves(a, m):
    n, w = a.shape
    a4 = a.reshape(n // (2 * m), 2, m, w)
    return a4[:, 0], a4[:, 1]


def _merge_halves(lo, hi):
    nb, m, w = lo.shape
    return jnp.stack([lo, hi], axis=1).reshape(2 * nb * m, w)


def _block_diag(a):
    z = jnp.zeros((a.shape[0], LANES), a.dtype)
    return jnp.concatenate([jnp.concatenate([a[:, :LANES], z], axis=1),
                            jnp.concatenate([z, a[:, LANES:]], axis=1)], axis=0)


def _hgrn2_pair(hp, lb_all, row, proj_ref, hgnw_ref, st_ref, lv_ref, y_ref, rows, fill, fill_levels):
    L = proj_ref.shape[0]
    nlev = int(math.log2(L))
    width = 2 * HG_E
    sl = slice(hp * width, (hp + 1) * width)
    lb = lb_all[:, sl]
    q = proj_ref[:, COL_HQ + hp * width:COL_HQ + (hp + 1) * width]
    v = proj_ref[:, COL_HI + hp * width:COL_HI + (hp + 1) * width]
    s_pos, s_neg = _sigmoid_pair(proj_ref[:, COL_HF + hp * width:COL_HF + (hp + 1) * width])
    kk = (1.0 - lb) * s_neg
    c = jnp.log(lb + (1.0 - lb) * s_pos)

    att = None
    for lev in range(nlev):
        m = 1 << lev
        if m < SUBLANES:
            upper = ((row >> lev) & 1) == 1
            tot = _block_last(c, m, row)
            e = jnp.exp(jnp.where(upper, c, tot - c))
            zz = (jnp.where(upper, q, kk) * e).astype(BF16)
            p = jnp.where(lv_ref[...] == lev, _dot_nt(zz, _block_diag(zz)), 0.0)
            att = p if att is None else att + p
            c = c + jnp.where(upper, pltpu.roll(tot, m, axis=0), 0.0)
        else:
            c_lo, c_hi = _split_halves(c, m)
            tot_lo = jnp.broadcast_to(c_lo[:, m - 1:m, :], c_lo.shape)
            z_lo = _split_halves(kk, m)[0] * jnp.exp(tot_lo - c_lo)
            z_hi = _split_halves(q, m)[1] * jnp.exp(c_hi)
            zz = _merge_halves(z_lo, z_hi).astype(BF16)
            p = _dot_nt(z_hi.reshape(L // 2, width).astype(BF16), _block_diag(zz))
            lv_hi = _split_halves(lv_ref[...], m)[1].reshape(L // 2, 2 * L)
            p = jnp.where(lv_hi == lev, p, 0.0).reshape(L // (2 * m), m, 2 * L)
            att_lo, att_hi = _split_halves(att, m)
            att = _merge_halves(att_lo, att_hi + p)
            c = _merge_halves(c_lo, c_hi + tot_lo)
        if lev in fill_levels:
            fill()

    tot = _block_last(c, L, row)
    st_old = st_ref[hp]
    vb = v.astype(BF16)
    qk_diag = q * kk
    o = (_dot(att.astype(BF16), _block_diag(vb))
         + _dot_nt((q * jnp.exp(c)).astype(BF16), _block_diag(st_old.astype(BF16))))
    ke = (kk * jnp.exp(tot - c)).astype(BF16)
    upd = jnp.concatenate([_dot_tn(vb[:, :HG_DV], ke[:, :HG_E]),
                           _dot_tn(vb[:, HG_DV:], ke[:, HG_E:])], axis=1)
    st_ref[hp] = st_old * jnp.exp(tot[0:HG_DV, :]) + upd

    gz = proj_ref[:, COL_HG + hp * width:COL_HG + (hp + 1) * width]
    gate = _silu(gz)
    nw = hgnw_ref[:, sl]
    for i in range(2):
        hs = slice(i * HG_DV, (i + 1) * HG_DV)
        oi = o[:, hs] + jnp.sum(qk_diag[:, hs], axis=-1, keepdims=True) * v[:, hs]
        on = oi * _rms_scale(oi) * nw[:, hs]
        col = ML_WIDTH + hp * width + i * HG_DV
        y_ref[rows, col:col + HG_DV] = (on * gate[:, hs]).astype(y_ref.dtype)


def _mix(proj_ref, gate_ref, convw_ref, convb_ref, gbias_ref, mlnw_ref, hglb_ref, hgnw_ref,
         y_ref, r0, qkext_ref, c_ref, n_ref, m_ref, st_ref, lv_ref, tasks, late_task):
    L = proj_ref.shape[0]
    rows = slice(r0, r0 + L)
    fill = _Filler(tasks)

    fill(2)
    a8, ct = _mlstm_gates(gate_ref, gbias_ref, m_ref, L)
    fill(2)

    qkext_ref[SUBLANES:SUBLANES + L, :] = proj_ref[:, COL_QK:COL_QK + 2 * ML_QK]
    conv = convb_ref[...]
    for k in range(ML_CONV):
        off = SUBLANES - (ML_CONV - 1) + k
        conv = conv + qkext_ref[off:off + L, :] * convw_ref[k:k + 1, :]
    qkext_ref[0:SUBLANES, :] = qkext_ref[L:L + SUBLANES, :]
    qk = _silu(conv)
    q_all = qk[:, :ML_QK]
    k_all = qk[:, ML_QK:] * (ML_DQK ** -0.5)
    fill(2)

    row = lax.broadcasted_iota(jnp.int32, (L, 2 * HG_E), 0)
    lbl = hglb_ref[...]
    lmax = jnp.max(lbl, axis=0, keepdims=True)
    lexp = jnp.exp(lbl - lmax)
    lb_all = lexp[0:1, :] / jnp.sum(lexp, axis=0, keepdims=True)
    for hp in range(HG_HEADS // 2):
        _hgrn2_pair(hp, lb_all, row, proj_ref, hgnw_ref, st_ref, lv_ref, y_ref, rows, fill,
                    fill_levels=(0, 1, 3, 4, 6))
        if hp == HG_HEADS // 4 - 1:
            late_task()

    rowi = lax.broadcasted_iota(jnp.int32, (L, L), 0)
    coli = lax.broadcasted_iota(jnp.int32, (L, L), 1)
    causal = coli <= rowi
    for h in range(ML_HEADS):
        if h % 2 == 0:
            pair = slice(h * ML_DQK, (h + 2) * ML_DQK)
            qk_pair = _dot_nt(q_all[:, pair].astype(BF16), _block_diag(k_all[:, pair].astype(BF16)))
        qk_h = qk_pair[:, (h % 2) * L:(h % 2 + 1) * L]
        _mlstm_head(h, q_all, k_all, qk_h, a8, ct, causal, proj_ref, mlnw_ref, c_ref, n_ref, y_ref,
                    rows, fill)
        fill()
    fill.flush()


def _mixer_kernel(steps_per_seq,
                  x0_ref, xa_ref, xb_ref, nw_ref, w_ref, wg_ref,
                  convw_ref, convb_ref, gbias_ref, mlnw_ref, hglb_ref, hgnw_ref,
                  y_ref,
                  proj_a, proj_b, gate_a, gate_b, u_a, u_b,
                  qkext_ref, c_ref, n_ref, m_ref, st_ref, lv_ref):
    L = proj_a.shape[0]
    g = pl.program_id(0)

    @pl.when(g == 0)
    def _():
        _normalise(x0_ref[0:L, :], nw_ref, u_a)
        for task in _projection_tasks(u_a, w_ref, wg_ref, proj_a, gate_a):
            task()
        _normalise(x0_ref[L:2 * L, :], nw_ref, u_b)
        r = lax.broadcasted_iota(jnp.int32, (L, L), 0)
        c = lax.broadcasted_iota(jnp.int32, (L, L), 1)
        lv = 31 - lax.clz(r ^ c)
        lv = jnp.where(r > c, lv, -1)
        lv_ref[...] = jnp.concatenate([lv, lv], axis=1)

    @pl.when(g % steps_per_seq == 0)
    def _():
        qkext_ref[0:SUBLANES, :] = jnp.zeros((SUBLANES, 2 * ML_QK), F32)
        c_ref[...] = jnp.zeros(c_ref.shape, F32)
        n_ref[...] = jnp.zeros(n_ref.shape, F32)
        m_ref[...] = jnp.zeros(m_ref.shape, F32)
        st_ref[...] = jnp.zeros(st_ref.shape, F32)

    params = (convw_ref, convb_ref, gbias_ref, mlnw_ref, hglb_ref, hgnw_ref)
    state = (qkext_ref, c_ref, n_ref, m_ref, st_ref, lv_ref)
    _mix(proj_a, gate_a, *params, y_ref, 0, *state,
         _projection_tasks(u_b, w_ref, wg_ref, proj_b, gate_b),
         lambda: _normalise(xa_ref[...], nw_ref, u_a))
    _mix(proj_b, gate_b, *params, y_ref, L, *state,
         _projection_tasks(u_a, w_ref, wg_ref, proj_a, gate_a),
         lambda: _normalise(xb_ref[...], nw_ref, u_b))


def _mixers(x2, norm_w, w_main, w_gate, conv_w, conv_b, gbias, ml_norm_w, hg_lb,
            hg_norm_w, chunk, chunks_per_seq):
    t = x2.shape[0]
    L = chunk
    nchunks = t // L
    steps = nchunks // 2
    assert chunks_per_seq % 2 == 0
    const = lambda shape: pl.BlockSpec(shape, lambda g: (0, 0), pipeline_mode=pl.Buffered(1))
    return pl.pallas_call(
        functools.partial(_mixer_kernel, chunks_per_seq // 2),
        grid=(steps,),
        in_specs=[
            const((2 * L, D_MODEL)),
            pl.BlockSpec((L, D_MODEL), lambda g: (jnp.minimum(2 * g + 2, nchunks - 1), 0)),
            pl.BlockSpec((L, D_MODEL), lambda g: (jnp.minimum(2 * g + 3, nchunks - 1), 0)),
            const((1, D_MODEL)),
            pl.BlockSpec(w_main.shape, lambda g: (0, 0, 0), pipeline_mode=pl.Buffered(1)),
            const((GATE_ROWS, D_MODEL)),
            const((ML_CONV, 2 * ML_QK)),
            const((1, 2 * ML_QK)),
            const((GATE_ROWS, LANES)),
            const((1, ML_WIDTH)),
            const((2, HG_WIDTH)),
            const((1, HG_WIDTH)),
        ],
        out_specs=pl.BlockSpec((2 * L, ML_WIDTH + HG_WIDTH), lambda g: (g, 0)),
        out_shape=jax.ShapeDtypeStruct((t, ML_WIDTH + HG_WIDTH), BF16),
        scratch_shapes=[
            pltpu.VMEM((L, MAIN_COLS), F32),
            pltpu.VMEM((L, MAIN_COLS), F32),
            pltpu.VMEM((GATE_ROWS, L), F32),
            pltpu.VMEM((GATE_ROWS, L), F32),
            pltpu.VMEM((L, D_MODEL), BF16),
            pltpu.VMEM((L, D_MODEL), BF16),
            pltpu.VMEM((L + SUBLANES, 2 * ML_QK), F32),
            pltpu.VMEM((ML_HEADS, ML_DQK, ML_DV), F32),
            pltpu.VMEM((ML_HEADS, SUBLANES, ML_DQK), F32),
            pltpu.VMEM((SUBLANES, LANES), F32),
            pltpu.VMEM((HG_HEADS // 2, HG_DV, 2 * HG_E), F32),
            pltpu.VMEM((L, 2 * L), jnp.int32),
        ],
        compiler_params=pltpu.CompilerParams(
            dimension_semantics=("arbitrary",),
            vmem_limit_bytes=V7X_VMEM_LIMIT),
        name="mixers",
    )(x2, x2, x2, norm_w, w_main, w_gate, conv_w, conv_b, gbias, ml_norm_w, hg_lb, hg_norm_w)


def _out_kernel(x_ref, y_ref, p_ref, wout_ref, wpg_ref, wpe_ref, pnw_ref, fnw_ref, o_ref):
    h = x_ref[...] + _dot_tiled(y_ref[...], wout_ref)
    hn = h * _rms_scale(h) * pnw_ref[...]
    gate, _ = _sigmoid_pair(_dot_tiled(hn.astype(BF16), wpg_ref))
    emb = _dot_tiled(p_ref[...].astype(BF16), wpe_ref)
    h = h + gate * emb
    o_ref[...] = h * _rms_scale(h) * fnw_ref[...]


def _out_stage(x2, y, p2, w_out, w_pg, w_pe, pe_norm_w, final_norm_w):
    t = x2.shape[0]
    tm = min(TM_OUT, t)
    const = lambda shape: pl.BlockSpec(shape, lambda i: (0, 0), pipeline_mode=pl.Buffered(1))
    const3 = lambda shape: pl.BlockSpec(shape, lambda i: (0, 0, 0), pipeline_mode=pl.Buffered(1))
    return pl.pallas_call(
        _out_kernel,
        grid=(t // tm,),
        in_specs=[
            pl.BlockSpec((tm, D_MODEL), lambda i: (i, 0)),
            pl.BlockSpec((tm, D_MODEL), lambda i: (i, 0)),
            pl.BlockSpec((tm, PE_DIM), lambda i: (i, 0)),
            const3(w_out.shape),
            const3(w_pg.shape),
            const3(w_pe.shape),
            const((1, D_MODEL)),
            const((1, D_MODEL)),
        ],
        out_specs=pl.BlockSpec((tm, D_MODEL), lambda i: (i, 0)),
        out_shape=jax.ShapeDtypeStruct((t, D_MODEL), F32),
        compiler_params=pltpu.CompilerParams(
            dimension_semantics=("arbitrary",),
            vmem_limit_bytes=V7X_VMEM_LIMIT),
        name="out_stage",
    )(x2, y, p2, w_out, w_pg, w_pe, pe_norm_w, final_norm_w)


def kernel(x, p, norm_w, w_in, conv_w, conv_b, ml_b_i, ml_b_f, ml_norm_w, hg_lb, hg_norm_w,
           w_out, pe_norm_w, w_pg, w_pe, final_norm_w):
    batch, seq, _ = x.shape
    t = batch * seq
    chunk = min(CHUNK, seq)
    x2 = x.reshape(t, D_MODEL).astype(F32)
    p2 = p.reshape(-1, PE_DIM).astype(F32)

    w = w_in[0]
    n_gate = 2 * ML_HEADS
    gate_lo = 2 * ML_QK + 3 * ML_WIDTH
    w_main = jnp.concatenate([w[:, :gate_lo], w[:, gate_lo + n_gate:]], axis=1).astype(BF16)
    w_main = _col_tiles(w_main, TN_IN)
    w_gate = jnp.pad(w[:, gate_lo:gate_lo + n_gate].T, ((0, GATE_ROWS - n_gate), (0, 0))).astype(BF16)
    gbias = jnp.pad(jnp.concatenate([ml_b_i[0], ml_b_f[0]]).astype(F32), (0, GATE_ROWS - n_gate))
    gbias = jnp.broadcast_to(gbias[:, None], (GATE_ROWS, LANES))
    nw = norm_w[0].reshape(1, D_MODEL).astype(F32)

    y = _mixers(x2, nw, w_main, w_gate,
                conv_w[0].astype(F32), conv_b[0].reshape(1, -1).astype(F32), gbias,
                ml_norm_w[0].reshape(1, -1).astype(F32), hg_lb.astype(F32),
                hg_norm_w[0].reshape(1, -1).astype(F32), chunk, seq // chunk)
    out = _out_stage(x2, y, p2, _col_tiles(w_out[0].astype(BF16), TN_OUT),
                     _col_tiles(w_pg[0].astype(BF16), TN_OUT),
                     _col_tiles(w_pe[0].astype(BF16), TN_OUT),
                     pe_norm_w[0].reshape(1, -1).astype(F32),
                     final_norm_w.reshape(1, -1).astype(F32))
    return out.reshape(batch, seq, D_MODEL).astype(x.dtype)
```

```python
import functools
import math

import jax
import jax.numpy as jnp
from jax import lax
from jax.experimental import pallas as pl
from jax.experimental.pallas import tpu as pltpu

F32 = jnp.float32
BF16 = jnp.bfloat16

D_MODEL = 2048
PE_DIM = 256
ML_HEADS = 4
ML_DQK = 128
ML_DV = 256
ML_QK = ML_HEADS * ML_DQK
ML_WIDTH = ML_HEADS * ML_DV
ML_CONV = 4
HG_HEADS = 8
HG_E = 128
HG_DV = 128
HG_WIDTH = HG_HEADS * HG_DV
EPS = 1e-6

MAIN_COLS = 8192
COL_QK = 0
COL_V = 1024
COL_O = 2048
COL_Z = 3072
COL_HQ = 4096
COL_HF = 5120
COL_HI = 6144
COL_HG = 7168
GATE_ROWS = 16

LANES = 128
SUBLANES = 8
NEG_BIG = -1e30
V7X_VMEM_LIMIT = 58 * 1024 * 1024

CHUNK = 128
TN_IN = 256
TN_OUT = 512
TM_OUT = 512
OUT_SPLIT = 2


def _dot(a, b):
    return jnp.dot(a, b, preferred_element_type=F32)


def _dot_nt(a, b):
    return lax.dot_general(a, b, (((1,), (1,)), ((), ())), preferred_element_type=F32)


def _dot_tn(a, b):
    return lax.dot_general(a, b, (((0,), (0,)), ((), ())), preferred_element_type=F32)


def _col_tiles(w, tn):
    k, n = w.shape
    return w.reshape(k, n // tn, tn).transpose(1, 0, 2)


def _dot_tiled(a, w_ref):
    return jnp.concatenate([_dot(a, w_ref[j]) for j in range(w_ref.shape[0])], axis=1)


def _rms_scale(x):
    return lax.rsqrt(jnp.mean(x * x, axis=-1, keepdims=True) + EPS)


def _scan_lanes(x, op, fill):
    n = x.shape[1]
    lane = lax.broadcasted_iota(jnp.int32, x.shape, 1)
    s = 1
    while s < n:
        shifted = pltpu.roll(x, s, axis=1)
        x = op(x, jnp.where(lane >= s, shifted, fill))
        s *= 2
    return x


def _sigmoid(x):
    return 1.0 / (1.0 + jnp.exp(-x))


def _silu(x):
    return x * _sigmoid(x)


def _normalise(x, nw_ref, u_ref):
    u_ref[...] = (x * _rms_scale(x) * nw_ref[...]).astype(BF16)


def _projection_tasks(u_ref, w_ref, wg_ref, proj_dst, gate_dst):
    n_tiles, _, tn = w_ref.shape

    def gates():
        gate_dst[...] = _dot_nt(wg_ref[...], u_ref[...])

    def tile(j):
        proj_dst[:, j * tn:(j + 1) * tn] = _dot(u_ref[...], w_ref[j])

    return [gates] + [functools.partial(tile, j) for j in range(n_tiles)]


class _Filler:
    def __init__(self, tasks):
        self._tasks = list(tasks)

    def __call__(self, n=1):
        for _ in range(n):
            if self._tasks:
                self._tasks.pop(0)()

    def flush(self):
        self(len(self._tasks))


def _mlstm_gates(gate_ref, gbias_ref, m_ref, L):
    i8 = gate_ref[0:SUBLANES, :] + gbias_ref[0:SUBLANES, 0:1]
    f8 = pltpu.roll(i8, ML_HEADS, axis=0)
    lf8 = jnp.minimum(f8, 0.0) - jnp.log1p(jnp.exp(-jnp.abs(f8)))
    b8 = _scan_lanes(lf8, jnp.add, 0.0)
    a8 = i8 - b8
    cm8 = _scan_lanes(a8, jnp.maximum, NEG_BIG)
    mprev8 = m_ref[:, 0:1]
    mm8 = jnp.maximum(mprev8, cm8)
    mlast8 = jnp.max(mm8, axis=1, keepdims=True)
    g8 = jnp.sum(lf8, axis=1, keepdims=True)
    m_ref[...] = jnp.broadcast_to(g8 + mlast8, m_ref.shape)
    table = jnp.concatenate(
        [mm8, b8, a8,
         jnp.broadcast_to(mlast8, (SUBLANES, L)),
         jnp.broadcast_to(mprev8, (SUBLANES, L)),
         jnp.zeros((LANES - 5 * SUBLANES, L), F32)], axis=0)
    return a8, table.T


def _mlstm_head(h, q_all, k_all, qk_h, a8, ct, causal, proj_ref, mlnw_ref, c_ref, n_ref, y_ref, rows,
                fill):
    mm_c = ct[:, h:h + 1]
    b_c = ct[:, 8 + h:9 + h]
    a_c = ct[:, 16 + h:17 + h]
    mlast_c = ct[:, 24 + h:25 + h]
    mprev_c = ct[:, 32 + h:33 + h]
    a_r = a8[h:h + 1, :]

    qh = q_all[:, h * ML_DQK:(h + 1) * ML_DQK]
    kh = k_all[:, h * ML_DQK:(h + 1) * ML_DQK]
    qb = qh.astype(BF16)
    vb = proj_ref[:, COL_V + h * ML_DV:COL_V + (h + 1) * ML_DV].astype(BF16)

    decay = jnp.exp(jnp.where(causal, a_r - mm_c, NEG_BIG))
    scores = qk_h * decay
    w_inter = jnp.exp(mprev_c - mm_c)
    c_old = c_ref[h]
    n_old = n_ref[h]
    num = _dot(scores.astype(BF16), vb) + w_inter * _dot(qb, c_old.astype(BF16))
    den = (jnp.sum(scores, axis=-1, keepdims=True)
           + w_inter * jnp.sum(qh * n_old[0:1, :], axis=-1, keepdims=True))
    hh = num / jnp.maximum(jnp.abs(den), jnp.exp(-(b_c + mm_c)))
    fill()

    wa_c = jnp.exp(a_c - mlast_c)
    ws_c = jnp.exp(mprev_c - mlast_c)
    kw = kh * wa_c
    c_ref[h] = ws_c[0:ML_DQK, :] * c_old + _dot_tn(kw.astype(BF16), vb)
    n_ref[h] = ws_c[0:SUBLANES, :] * n_old + jnp.sum(kw, axis=0, keepdims=True)

    hn = hh * _rms_scale(hh) * mlnw_ref[:, h * ML_DV:(h + 1) * ML_DV]
    o_gate = _sigmoid(proj_ref[:, COL_O + h * ML_DV:COL_O + (h + 1) * ML_DV])
    z = proj_ref[:, COL_Z + h * ML_DV:COL_Z + (h + 1) * ML_DV]
    y_ref[rows, h * ML_DV:(h + 1) * ML_DV] = (o_gate * hn * _silu(z)).astype(y_ref.dtype)


def _split_halves(a, m):
    n, w = a.shape
    a4 = a.reshape(n // (2 * m), 2, m, w)
    return a4[:, 0], a4[:, 1]


def _merge_halves(lo, hi):
    nb, m, w = lo.shape
    return jnp.stack([lo, hi], axis=1).reshape(2 * nb * m, w)


def _block_diag(a):
    z = jnp.zeros((a.shape[0], LANES), a.dtype)
    return jnp.concatenate([jnp.concatenate([a[:, :LANES], z], axis=1),
                            jnp.concatenate([z, a[:, LANES:]], axis=1)], axis=0)


def _hgrn2_pair(hp, lb_all, proj_ref, hgnw_ref, st_ref, lv_ref, y_ref, rows, fill, fill_levels):
    L = proj_ref.shape[0]
    nlev = int(math.log2(L))
    width = 2 * HG_E
    sl = slice(hp * width, (hp + 1) * width)
    lb = lb_all[:, sl]
    q = proj_ref[:, COL_HQ + hp * width:COL_HQ + (hp + 1) * width]
    v = proj_ref[:, COL_HI + hp * width:COL_HI + (hp + 1) * width]
    f_pre = proj_ref[:, COL_HF + hp * width:COL_HF + (hp + 1) * width]
    kk = (1.0 - lb) * _sigmoid(-f_pre)
    c = jnp.log(lb + (1.0 - lb) * _sigmoid(f_pre))

    att = None
    q8 = q.reshape(L // SUBLANES, SUBLANES, width)
    kk8 = kk.reshape(L // SUBLANES, SUBLANES, width)
    sub = lax.broadcasted_iota(jnp.int32, q8.shape, 1)
    for lev in range(nlev):
        m = 1 << lev
        if m < SUBLANES:
            c8 = c.reshape(L // SUBLANES, SUBLANES, width)
            upper = (sub & m) != 0
            tot = c8
            s = 1
            while s < m:
                tot = jnp.where((sub & s) != 0, tot, pltpu.roll(tot, SUBLANES - s, axis=1))
                s *= 2
            e = jnp.exp(jnp.where(upper, c8, tot - c8))
            zz = (jnp.where(upper, q8, kk8) * e).reshape(L, width).astype(BF16)
            p = jnp.where(lv_ref[...] == lev, _dot_nt(zz, _block_diag(zz)), 0.0)
            att = p if att is None else att + p
            c = (c8 + jnp.where(upper, pltpu.roll(tot, m, axis=1), 0.0)).reshape(L, width)
        else:
            c_lo, c_hi = _split_halves(c, m)
            tot_lo = jnp.broadcast_to(c_lo[:, m - 1:m, :], c_lo.shape)
            z_lo = _split_halves(kk, m)[0] * jnp.exp(tot_lo - c_lo)
            z_hi = _split_halves(q, m)[1] * jnp.exp(c_hi)
            zz = _merge_halves(z_lo, z_hi).astype(BF16)
            p = _dot_nt(z_hi.reshape(L // 2, width).astype(BF16), _block_diag(zz))
            lv_hi = _split_halves(lv_ref[...], m)[1].reshape(L // 2, 2 * L)
            p = jnp.where(lv_hi == lev, p, 0.0).reshape(L // (2 * m), m, 2 * L)
            att_lo, att_hi = _split_halves(att, m)
            att = _merge_halves(att_lo, att_hi + p)
            c = _merge_halves(c_lo, c_hi + tot_lo)
        if lev in fill_levels:
            fill()

    tot = jnp.broadcast_to(c[L - 1:L, :], c.shape)
    st_old = st_ref[hp]
    vb = v.astype(BF16)
    qk_diag = q * kk
    o = (_dot(att.astype(BF16), _block_diag(vb))
         + _dot_nt((q * jnp.exp(c)).astype(BF16), _block_diag(st_old.astype(BF16))))
    ke = (kk * jnp.exp(tot - c)).astype(BF16)
    upd = jnp.concatenate([_dot_tn(vb[:, :HG_DV], ke[:, :HG_E]),
                           _dot_tn(vb[:, HG_DV:], ke[:, HG_E:])], axis=1)
    st_ref[hp] = st_old * jnp.exp(tot[0:HG_DV, :]) + upd

    gz = proj_ref[:, COL_HG + hp * width:COL_HG + (hp + 1) * width]
    gate = _silu(gz)
    nw = hgnw_ref[:, sl]
    for i in range(2):
        hs = slice(i * HG_DV, (i + 1) * HG_DV)
        oi = o[:, hs] + jnp.sum(qk_diag[:, hs], axis=-1, keepdims=True) * v[:, hs]
        on = oi * _rms_scale(oi) * nw[:, hs]
        col = ML_WIDTH + hp * width + i * HG_DV
        y_ref[rows, col:col + HG_DV] = (on * gate[:, hs]).astype(y_ref.dtype)


def _mix(proj_ref, gate_ref, convw_ref, convb_ref, gbias_ref, mlnw_ref, hglb_ref, hgnw_ref,
         y_ref, r0, qkext_ref, c_ref, n_ref, m_ref, st_ref, lv_ref, tasks, late_task):
    L = proj_ref.shape[0]
    rows = slice(r0, r0 + L)
    fill = _Filler(tasks)

    fill(2)
    a8, ct = _mlstm_gates(gate_ref, gbias_ref, m_ref, L)
    fill(2)

    qkext_ref[SUBLANES:SUBLANES + L, :] = proj_ref[:, COL_QK:COL_QK + 2 * ML_QK]
    conv = convb_ref[...]
    for k in range(ML_CONV):
        off = SUBLANES - (ML_CONV - 1) + k
        conv = conv + qkext_ref[off:off + L, :] * convw_ref[k:k + 1, :]
    qkext_ref[0:SUBLANES, :] = qkext_ref[L:L + SUBLANES, :]
    qk = _silu(conv)
    q_all = qk[:, :ML_QK]
    k_all = qk[:, ML_QK:] * (ML_DQK ** -0.5)
    fill(2)

    lbl = hglb_ref[...]
    lmax = jnp.max(lbl, axis=0, keepdims=True)
    lexp = jnp.exp(lbl - lmax)
    lb_all = lexp[0:1, :] / jnp.sum(lexp, axis=0, keepdims=True)
    for hp in range(HG_HEADS // 2):
        _hgrn2_pair(hp, lb_all, proj_ref, hgnw_ref, st_ref, lv_ref, y_ref, rows, fill,
                    fill_levels=(0, 1, 3, 4, 6))
        if hp == HG_HEADS // 4 - 1:
            late_task()

    rowi = lax.broadcasted_iota(jnp.int32, (L, L), 0)
    coli = lax.broadcasted_iota(jnp.int32, (L, L), 1)
    causal = coli <= rowi
    for h in range(ML_HEADS):
        if h % 2 == 0:
            pair = slice(h * ML_DQK, (h + 2) * ML_DQK)
            qk_pair = _dot_nt(q_all[:, pair].astype(BF16), _block_diag(k_all[:, pair].astype(BF16)))
        qk_h = qk_pair[:, (h % 2) * L:(h % 2 + 1) * L]
        _mlstm_head(h, q_all, k_all, qk_h, a8, ct, causal, proj_ref, mlnw_ref, c_ref, n_ref, y_ref,
                    rows, fill)
        fill()
    fill.flush()


def _mixer_kernel(steps_per_seq,
                  x0_ref, xa_ref, xb_ref, nw_ref, w_ref, wg_ref,
                  convw_ref, convb_ref, gbias_ref, mlnw_ref, hglb_ref, hgnw_ref,
                  y_ref,
                  proj_a, proj_b, gate_a, gate_b, u_a, u_b,
                  qkext_ref, c_ref, n_ref, m_ref, st_ref, lv_ref):
    L = proj_a.shape[0]
    g = pl.program_id(0)

    @pl.when(g == 0)
    def _():
        _normalise(x0_ref[0:L, :], nw_ref, u_a)
        for task in _projection_tasks(u_a, w_ref, wg_ref, proj_a, gate_a):
            task()
        _normalise(x0_ref[L:2 * L, :], nw_ref, u_b)
        r = lax.broadcasted_iota(jnp.int32, (L, L), 0)
        c = lax.broadcasted_iota(jnp.int32, (L, L), 1)
        lv = 31 - lax.clz(r ^ c)
        lv = jnp.where(r > c, lv, -1)
        lv_ref[...] = jnp.concatenate([lv, lv], axis=1)

    @pl.when(g % steps_per_seq == 0)
    def _():
        qkext_ref[0:SUBLANES, :] = jnp.zeros((SUBLANES, 2 * ML_QK), F32)
        c_ref[...] = jnp.zeros(c_ref.shape, F32)
        n_ref[...] = jnp.zeros(n_ref.shape, F32)
        m_ref[...] = jnp.zeros(m_ref.shape, F32)
        st_ref[...] = jnp.zeros(st_ref.shape, F32)

    params = (convw_ref, convb_ref, gbias_ref, mlnw_ref, hglb_ref, hgnw_ref)
    state = (qkext_ref, c_ref, n_ref, m_ref, st_ref, lv_ref)
    _mix(proj_a, gate_a, *params, y_ref, 0, *state,
         _projection_tasks(u_b, w_ref, wg_ref, proj_b, gate_b),
         lambda: _normalise(xa_ref[...], nw_ref, u_a))
    _mix(proj_b, gate_b, *params, y_ref, L, *state,
         _projection_tasks(u_a, w_ref, wg_ref, proj_a, gate_a),
         lambda: _normalise(xb_ref[...], nw_ref, u_b))


def _mixers(x2, norm_w, w_main, w_gate, conv_w, conv_b, gbias, ml_norm_w, hg_lb,
            hg_norm_w, chunk, chunks_per_seq):
    t = x2.shape[0]
    L = chunk
    nchunks = t // L
    steps = nchunks // 2
    assert chunks_per_seq % 2 == 0
    const = lambda shape: pl.BlockSpec(shape, lambda g: (0, 0), pipeline_mode=pl.Buffered(1))
    return pl.pallas_call(
        functools.partial(_mixer_kernel, chunks_per_seq // 2),
        grid=(steps,),
        in_specs=[
            const((2 * L, D_MODEL)),
            pl.BlockSpec((L, D_MODEL), lambda g: (jnp.minimum(2 * g + 2, nchunks - 1), 0)),
            pl.BlockSpec((L, D_MODEL), lambda g: (jnp.minimum(2 * g + 3, nchunks - 1), 0)),
            const((1, D_MODEL)),
            pl.BlockSpec(w_main.shape, lambda g: (0, 0, 0), pipeline_mode=pl.Buffered(1)),
            const((GATE_ROWS, D_MODEL)),
            const((ML_CONV, 2 * ML_QK)),
            const((1, 2 * ML_QK)),
            const((GATE_ROWS, LANES)),
            const((1, ML_WIDTH)),
            const((2, HG_WIDTH)),
            const((1, HG_WIDTH)),
        ],
        out_specs=pl.BlockSpec((2 * L, ML_WIDTH + HG_WIDTH), lambda g: (g, 0)),
        out_shape=jax.ShapeDtypeStruct((t, ML_WIDTH + HG_WIDTH), BF16),
        scratch_shapes=[
            pltpu.VMEM((L, MAIN_COLS), F32),
            pltpu.VMEM((L, MAIN_COLS), F32),
            pltpu.VMEM((GATE_ROWS, L), F32),
            pltpu.VMEM((GATE_ROWS, L), F32),
            pltpu.VMEM((L, D_MODEL), BF16),
            pltpu.VMEM((L, D_MODEL), BF16),
            pltpu.VMEM((L + SUBLANES, 2 * ML_QK), F32),
            pltpu.VMEM((ML_HEADS, ML_DQK, ML_DV), F32),
            pltpu.VMEM((ML_HEADS, SUBLANES, ML_DQK), F32),
            pltpu.VMEM((SUBLANES, LANES), F32),
            pltpu.VMEM((HG_HEADS // 2, HG_DV, 2 * HG_E), F32),
            pltpu.VMEM((L, 2 * L), jnp.int32),
        ],
        compiler_params=pltpu.CompilerParams(
            dimension_semantics=("arbitrary",),
            vmem_limit_bytes=V7X_VMEM_LIMIT),
        name="mixers",
    )(x2, x2, x2, norm_w, w_main, w_gate, conv_w, conv_b, gbias, ml_norm_w, hg_lb, hg_norm_w)


def _out_kernel(x_ref, y_ref, p_ref, wout_ref, wpg_ref, wpe_ref, pnw_ref, fnw_ref, o_ref):
    half = x_ref.shape[0] // OUT_SPLIT
    parts = [slice(i * half, (i + 1) * half) for i in range(OUT_SPLIT)]
    hs = [x_ref[rows, :] + _dot_tiled(y_ref[rows, :], wout_ref) for rows in parts]
    for rows, h in zip(parts, hs):
        hn = h * _rms_scale(h) * pnw_ref[...]
        gate = _sigmoid(_dot_tiled(hn.astype(BF16), wpg_ref))
        emb = _dot_tiled(p_ref[rows, :].astype(BF16), wpe_ref)
        h = h + gate * emb
        o_ref[rows, :] = h * _rms_scale(h) * fnw_ref[...]


def _out_stage(x2, y, p2, w_out, w_pg, w_pe, pe_norm_w, final_norm_w):
    t = x2.shape[0]
    tm = min(TM_OUT, t)
    const = lambda shape: pl.BlockSpec(shape, lambda i: (0, 0), pipeline_mode=pl.Buffered(1))
    const3 = lambda shape: pl.BlockSpec(shape, lambda i: (0, 0, 0), pipeline_mode=pl.Buffered(1))
    return pl.pallas_call(
        _out_kernel,
        grid=(t // tm,),
        in_specs=[
            pl.BlockSpec((tm, D_MODEL), lambda i: (i, 0)),
            pl.BlockSpec((tm, D_MODEL), lambda i: (i, 0)),
            pl.BlockSpec((tm, PE_DIM), lambda i: (i, 0)),
            const3(w_out.shape),
            const3(w_pg.shape),
            const3(w_pe.shape),
            const((1, D_MODEL)),
            const((1, D_MODEL)),
        ],
        out_specs=pl.BlockSpec((tm, D_MODEL), lambda i: (i, 0)),
        out_shape=jax.ShapeDtypeStruct((t, D_MODEL), F32),
        compiler_params=pltpu.CompilerParams(
            dimension_semantics=("arbitrary",),
            vmem_limit_bytes=V7X_VMEM_LIMIT),
        name="out_stage",
    )(x2, y, p2, w_out, w_pg, w_pe, pe_norm_w, final_norm_w)


def kernel(x, p, norm_w, w_in, conv_w, conv_b, ml_b_i, ml_b_f, ml_norm_w, hg_lb, hg_norm_w,
           w_out, pe_norm_w, w_pg, w_pe, final_norm_w):
    batch, seq, _ = x.shape
    t = batch * seq
    chunk = min(CHUNK, seq)
    x2 = x.reshape(t, D_MODEL).astype(F32)
    p2 = p.reshape(-1, PE_DIM).astype(F32)

    w = w_in[0]
    n_gate = 2 * ML_HEADS
    gate_lo = 2 * ML_QK + 3 * ML_WIDTH
    w_main = jnp.concatenate([w[:, :gate_lo], w[:, gate_lo + n_gate:]], axis=1).astype(BF16)
    w_main = _col_tiles(w_main, TN_IN)
    w_gate = jnp.pad(w[:, gate_lo:gate_lo + n_gate].T, ((0, GATE_ROWS - n_gate), (0, 0))).astype(BF16)
    gbias = jnp.pad(jnp.concatenate([ml_b_i[0], ml_b_f[0]]).astype(F32), (0, GATE_ROWS - n_gate))
    gbias = jnp.broadcast_to(gbias[:, None], (GATE_ROWS, LANES))
    nw = norm_w[0].reshape(1, D_MODEL).astype(F32)

    y = _mixers(x2, nw, w_main, w_gate,
                conv_w[0].astype(F32), conv_b[0].reshape(1, -1).astype(F32), gbias,
                ml_norm_w[0].reshape(1, -1).astype(F32), hg_lb.astype(F32),
                hg_norm_w[0].reshape(1, -1).astype(F32), chunk, seq // chunk)
    out = _out_stage(x2, y, p2, _col_tiles(w_out[0].astype(BF16), TN_OUT),
                     _col_tiles(w_pg[0].astype(BF16), TN_OUT),
                     _col_tiles(w_pe[0].astype(BF16), TN_OUT),
                     pe_norm_w[0].reshape(1, -1).astype(F32),
                     final_norm_w.reshape(1, -1).astype(F32))
    return out.reshape(batch, seq, D_MODEL).astype(x.dtype)
```

```python
import functools
import math

import jax
import jax.numpy as jnp
from jax import lax
from jax.experimental import pallas as pl
from jax.experimental.pallas import tpu as pltpu

F32 = jnp.float32
BF16 = jnp.bfloat16

D_MODEL = 2048
PE_DIM = 256
ML_HEADS = 4
ML_DQK = 128
ML_DV = 256
ML_QK = ML_HEADS * ML_DQK
ML_WIDTH = ML_HEADS * ML_DV
ML_CONV = 4
HG_HEADS = 8
HG_E = 128
HG_DV = 128
HG_WIDTH = HG_HEADS * HG_DV
EPS = 1e-6

MAIN_COLS = 8192
COL_QK = 0
COL_V = 1024
COL_O = 2048
COL_Z = 3072
COL_HQ = 4096
COL_HF = 5120
COL_HI = 6144
COL_HG = 7168
GATE_ROWS = 16

LANES = 128
SUBLANES = 8
NEG_BIG = -1e30
V7X_VMEM_LIMIT = 58 * 1024 * 1024

CHUNK = 128
TN_IN = 512
TM_OUT = 512
OUT_SPLIT = 2


def _dot(a, b):
    return jnp.dot(a, b, preferred_element_type=F32)


def _dot_nt(a, b):
    return lax.dot_general(a, b, (((1,), (1,)), ((), ())), preferred_element_type=F32)


def _dot_tn(a, b):
    return lax.dot_general(a, b, (((0,), (0,)), ((), ())), preferred_element_type=F32)


def _col_tiles(w, tn):
    k, n = w.shape
    return w.reshape(k, n // tn, tn).transpose(1, 0, 2)


def _rms_scale(x):
    return lax.rsqrt(jnp.mean(x * x, axis=-1, keepdims=True) + EPS)


def _scan_lanes(x, op, fill):
    n = x.shape[1]
    lane = lax.broadcasted_iota(jnp.int32, x.shape, 1)
    s = 1
    while s < n:
        shifted = pltpu.roll(x, s, axis=1)
        x = op(x, jnp.where(lane >= s, shifted, fill))
        s *= 2
    return x


def _sigmoid(x):
    return 1.0 / (1.0 + jnp.exp(-x))


def _silu(x):
    return x * _sigmoid(x)


def _normalise(x, nw_ref, u_ref):
    u_ref[...] = (x * _rms_scale(x) * nw_ref[...]).astype(BF16)


def _projection_tasks(u_ref, w_ref, wg_ref, proj_dst, gate_dst):
    n_tiles, _, tn = w_ref.shape

    def gates():
        gate_dst[...] = _dot_nt(wg_ref[...], u_ref[...])

    def tile(j):
        proj_dst[:, j * tn:(j + 1) * tn] = _dot(u_ref[...], w_ref[j])

    return [gates] + [functools.partial(tile, j) for j in range(n_tiles)]


class _Filler:
    def __init__(self, tasks, slots):
        self._tasks = list(tasks)
        self._total = len(self._tasks)
        self._slots = slots
        self._used = 0

    def __call__(self, n=1):
        self._used += n
        due = -(-self._total * self._used // self._slots)
        while self._total - len(self._tasks) < due and self._tasks:
            self._tasks.pop(0)()

    def flush(self):
        while self._tasks:
            self._tasks.pop(0)()


def _mlstm_gates(gate_ref, gbias_ref, m_ref, L):
    i8 = gate_ref[0:SUBLANES, :] + gbias_ref[0:SUBLANES, 0:1]
    f8 = pltpu.roll(i8, ML_HEADS, axis=0)
    lf8 = jnp.minimum(f8, 0.0) - jnp.log1p(jnp.exp(-jnp.abs(f8)))
    b8 = _scan_lanes(lf8, jnp.add, 0.0)
    a8 = i8 - b8
    cm8 = _scan_lanes(a8, jnp.maximum, NEG_BIG)
    mprev8 = m_ref[:, 0:1]
    mm8 = jnp.maximum(mprev8, cm8)
    mlast8 = jnp.max(mm8, axis=1, keepdims=True)
    g8 = jnp.sum(lf8, axis=1, keepdims=True)
    m_ref[...] = jnp.broadcast_to(g8 + mlast8, m_ref.shape)
    table = jnp.concatenate(
        [mm8, b8, a8,
         jnp.broadcast_to(mlast8, (SUBLANES, L)),
         jnp.broadcast_to(mprev8, (SUBLANES, L)),
         jnp.zeros((LANES - 5 * SUBLANES, L), F32)], axis=0)
    return a8, table.T


def _mlstm_head(h, q_all, k_all, qk_h, a8, ct, causal, proj_ref, mlnw_ref, c_ref, n_ref, y_ref, rows,
                fill):
    mm_c = ct[:, h:h + 1]
    b_c = ct[:, 8 + h:9 + h]
    a_c = ct[:, 16 + h:17 + h]
    mlast_c = ct[:, 24 + h:25 + h]
    mprev_c = ct[:, 32 + h:33 + h]
    a_r = a8[h:h + 1, :]

    qh = q_all[:, h * ML_DQK:(h + 1) * ML_DQK]
    kh = k_all[:, h * ML_DQK:(h + 1) * ML_DQK]
    vb = proj_ref[:, COL_V + h * ML_DV:COL_V + (h + 1) * ML_DV].astype(BF16)

    decay = jnp.exp(jnp.where(causal, a_r - mm_c, NEG_BIG))
    scores = qk_h * decay
    w_inter = jnp.exp(mprev_c - mm_c)
    c_old = c_ref[h]
    n_old = n_ref[h]
    lhs = jnp.concatenate([scores.astype(BF16), (w_inter * qh).astype(BF16)], axis=1)
    num = _dot(lhs, jnp.concatenate([vb, c_old.astype(BF16)], axis=0))
    den = (jnp.sum(scores, axis=-1, keepdims=True)
           + w_inter * jnp.sum(qh * n_old[0:1, :], axis=-1, keepdims=True))
    hh = num / jnp.maximum(jnp.abs(den), jnp.exp(-(b_c + mm_c)))
    fill()

    wa_c = jnp.exp(a_c - mlast_c)
    ws_c = jnp.exp(mprev_c - mlast_c)
    kw = kh * wa_c
    c_ref[h] = ws_c[0:ML_DQK, :] * c_old + _dot_tn(kw.astype(BF16), vb)
    n_ref[h] = ws_c[0:SUBLANES, :] * n_old + jnp.sum(kw, axis=0, keepdims=True)

    hn = hh * _rms_scale(hh) * mlnw_ref[:, h * ML_DV:(h + 1) * ML_DV]
    o_gate = _sigmoid(proj_ref[:, COL_O + h * ML_DV:COL_O + (h + 1) * ML_DV])
    z = proj_ref[:, COL_Z + h * ML_DV:COL_Z + (h + 1) * ML_DV]
    y_ref[rows, h * ML_DV:(h + 1) * ML_DV] = (o_gate * hn * _silu(z)).astype(y_ref.dtype)


def _split_halves(a, m):
    n, w = a.shape
    a4 = a.reshape(n // (2 * m), 2, m, w)
    return a4[:, 0], a4[:, 1]


def _merge_halves(lo, hi):
    nb, m, w = lo.shape
    return jnp.stack([lo, hi], axis=1).reshape(2 * nb * m, w)


def _block_diag(a):
    z = jnp.zeros((a.shape[0], LANES), a.dtype)
    return jnp.concatenate([jnp.concatenate([a[:, :LANES], z], axis=1),
                            jnp.concatenate([z, a[:, LANES:]], axis=1)], axis=0)


def _hgrn2_pair(hp, lb_all, proj_ref, hgnw_ref, st_ref, lv_ref, y_ref, rows, fill, fill_levels):
    L = proj_ref.shape[0]
    nlev = int(math.log2(L))
    width = 2 * HG_E
    sl = slice(hp * width, (hp + 1) * width)
    lb = lb_all[:, sl]
    q = proj_ref[:, COL_HQ + hp * width:COL_HQ + (hp + 1) * width]
    v = proj_ref[:, COL_HI + hp * width:COL_HI + (hp + 1) * width]
    f_pre = proj_ref[:, COL_HF + hp * width:COL_HF + (hp + 1) * width]
    kk = (1.0 - lb) * _sigmoid(-f_pre)
    c = jnp.log(lb + (1.0 - lb) * _sigmoid(f_pre))

    att = None
    q8 = q.reshape(L // SUBLANES, SUBLANES, width)
    kk8 = kk.reshape(L // SUBLANES, SUBLANES, width)
    sub = lax.broadcasted_iota(jnp.int32, q8.shape, 1)
    for lev in range(nlev):
        m = 1 << lev
        if m < SUBLANES:
            c8 = c.reshape(L // SUBLANES, SUBLANES, width)
            upper = (sub & m) != 0
            tot = c8
            s = 1
            while s < m:
                tot = jnp.where((sub & s) != 0, tot, pltpu.roll(tot, SUBLANES - s, axis=1))
                s *= 2
            e = jnp.exp(jnp.where(upper, c8, tot - c8))
            zz = (jnp.where(upper, q8, kk8) * e).reshape(L, width).astype(BF16)
            p = jnp.where(lv_ref[...] == lev, _dot_nt(zz, _block_diag(zz)), 0.0)
            att = p if att is None else att + p
            c = (c8 + jnp.where(upper, pltpu.roll(tot, m, axis=1), 0.0)).reshape(L, width)
        else:
            c_lo, c_hi = _split_halves(c, m)
            tot_lo = jnp.broadcast_to(c_lo[:, m - 1:m, :], c_lo.shape)
            z_lo = _split_halves(kk, m)[0] * jnp.exp(tot_lo - c_lo)
            z_hi = _split_halves(q, m)[1] * jnp.exp(c_hi)
            zz = _merge_halves(z_lo, z_hi).astype(BF16)
            p = _dot_nt(z_hi.reshape(L // 2, width).astype(BF16), _block_diag(zz))
            lv_hi = _split_halves(lv_ref[...], m)[1].reshape(L // 2, 2 * L)
            p = jnp.where(lv_hi == lev, p, 0.0).reshape(L // (2 * m), m, 2 * L)
            att_lo, att_hi = _split_halves(att, m)
            att = _merge_halves(att_lo, att_hi + p)
            c = _merge_halves(c_lo, c_hi + tot_lo)
        if lev in fill_levels:
            fill()

    tot = jnp.broadcast_to(c[L - 1:L, :], c.shape)
    st_old = st_ref[hp]
    vb = v.astype(BF16)
    qk_diag = q * kk
    o = (_dot(att.astype(BF16), _block_diag(vb))
         + _dot_nt((q * jnp.exp(c)).astype(BF16), _block_diag(st_old.astype(BF16))))
    ke = (kk * jnp.exp(tot - c)).astype(BF16)
    upd = jnp.concatenate([_dot_tn(vb[:, :HG_DV], ke[:, :HG_E]),
                           _dot_tn(vb[:, HG_DV:], ke[:, HG_E:])], axis=1)
    st_ref[hp] = st_old * jnp.exp(tot[0:HG_DV, :]) + upd

    gz = proj_ref[:, COL_HG + hp * width:COL_HG + (hp + 1) * width]
    gate = _silu(gz)
    nw = hgnw_ref[:, sl]
    for i in range(2):
        hs = slice(i * HG_DV, (i + 1) * HG_DV)
        oi = o[:, hs] + jnp.sum(qk_diag[:, hs], axis=-1, keepdims=True) * v[:, hs]
        on = oi * _rms_scale(oi) * nw[:, hs]
        col = ML_WIDTH + hp * width + i * HG_DV
        y_ref[rows, col:col + HG_DV] = (on * gate[:, hs]).astype(y_ref.dtype)


def _mix(proj_ref, gate_ref, convw_ref, convb_ref, gbias_ref, mlnw_ref, hglb_ref, hgnw_ref,
         y_ref, r0, qkext_ref, c_ref, n_ref, m_ref, st_ref, lv_ref, tasks, late_task):
    L = proj_ref.shape[0]
    rows = slice(r0, r0 + L)
    fill = _Filler(tasks, slots=6 + 5 * (HG_HEADS // 2) + 2 * ML_HEADS)

    fill(2)
    a8, ct = _mlstm_gates(gate_ref, gbias_ref, m_ref, L)
    fill(2)

    qkext_ref[SUBLANES:SUBLANES + L, :] = proj_ref[:, COL_QK:COL_QK + 2 * ML_QK]
    conv = convb_ref[...]
    for k in range(ML_CONV):
        off = SUBLANES - (ML_CONV - 1) + k
        conv = conv + qkext_ref[off:off + L, :] * convw_ref[k:k + 1, :]
    qkext_ref[0:SUBLANES, :] = qkext_ref[L:L + SUBLANES, :]
    qk = _silu(conv)
    q_all = qk[:, :ML_QK]
    k_all = qk[:, ML_QK:] * (ML_DQK ** -0.5)
    fill(2)

    lbl = hglb_ref[...]
    lmax = jnp.max(lbl, axis=0, keepdims=True)
    lexp = jnp.exp(lbl - lmax)
    lb_all = lexp[0:1, :] / jnp.sum(lexp, axis=0, keepdims=True)
    for hp in range(HG_HEADS // 2):
        _hgrn2_pair(hp, lb_all, proj_ref, hgnw_ref, st_ref, lv_ref, y_ref, rows, fill,
                    fill_levels=(0, 1, 3, 4, 6))
        if hp == HG_HEADS // 4 - 1:
            late_task()

    rowi = lax.broadcasted_iota(jnp.int32, (L, L), 0)
    coli = lax.broadcasted_iota(jnp.int32, (L, L), 1)
    causal = coli <= rowi
    for h in range(ML_HEADS):
        if h % 2 == 0:
            pair = slice(h * ML_DQK, (h + 2) * ML_DQK)
            qk_pair = _dot_nt(q_all[:, pair].astype(BF16), _block_diag(k_all[:, pair].astype(BF16)))
        qk_h = qk_pair[:, (h % 2) * L:(h % 2 + 1) * L]
        _mlstm_head(h, q_all, k_all, qk_h, a8, ct, causal, proj_ref, mlnw_ref, c_ref, n_ref, y_ref,
                    rows, fill)
        fill()
    fill.flush()


def _mixer_kernel(steps_per_seq,
                  x0_ref, xa_ref, xb_ref, nw_ref, w_ref, wg_ref,
                  convw_ref, convb_ref, gbias_ref, mlnw_ref, hglb_ref, hgnw_ref,
                  y_ref,
                  proj_a, proj_b, gate_a, gate_b, u_a, u_b,
                  qkext_ref, c_ref, n_ref, m_ref, st_ref, lv_ref):
    L = proj_a.shape[0]
    g = pl.program_id(0)

    @pl.when(g == 0)
    def _():
        _normalise(x0_ref[0:L, :], nw_ref, u_a)
        for task in _projection_tasks(u_a, w_ref, wg_ref, proj_a, gate_a):
            task()
        _normalise(x0_ref[L:2 * L, :], nw_ref, u_b)
        r = lax.broadcasted_iota(jnp.int32, (L, L), 0)
        c = lax.broadcasted_iota(jnp.int32, (L, L), 1)
        lv = 31 - lax.clz(r ^ c)
        lv = jnp.where(r > c, lv, -1)
        lv_ref[...] = jnp.concatenate([lv, lv], axis=1)

    @pl.when(g % steps_per_seq == 0)
    def _():
        qkext_ref[0:SUBLANES, :] = jnp.zeros((SUBLANES, 2 * ML_QK), F32)
        c_ref[...] = jnp.zeros(c_ref.shape, F32)
        n_ref[...] = jnp.zeros(n_ref.shape, F32)
        m_ref[...] = jnp.zeros(m_ref.shape, F32)
        st_ref[...] = jnp.zeros(st_ref.shape, F32)

    params = (convw_ref, convb_ref, gbias_ref, mlnw_ref, hglb_ref, hgnw_ref)
    state = (qkext_ref, c_ref, n_ref, m_ref, st_ref, lv_ref)
    _mix(proj_a, gate_a, *params, y_ref, 0, *state,
         _projection_tasks(u_b, w_ref, wg_ref, proj_b, gate_b),
         lambda: _normalise(xa_ref[...], nw_ref, u_a))
    _mix(proj_b, gate_b, *params, y_ref, L, *state,
         _projection_tasks(u_a, w_ref, wg_ref, proj_a, gate_a),
         lambda: _normalise(xb_ref[...], nw_ref, u_b))


def _mixers(x2, norm_w, w_main, w_gate, conv_w, conv_b, gbias, ml_norm_w, hg_lb,
            hg_norm_w, chunk, chunks_per_seq):
    t = x2.shape[0]
    L = chunk
    nchunks = t // L
    steps = nchunks // 2
    assert chunks_per_seq % 2 == 0
    const = lambda shape: pl.BlockSpec(shape, lambda g: (0, 0), pipeline_mode=pl.Buffered(1))
    return pl.pallas_call(
        functools.partial(_mixer_kernel, chunks_per_seq // 2),
        grid=(steps,),
        in_specs=[
            const((2 * L, D_MODEL)),
            pl.BlockSpec((L, D_MODEL), lambda g: (jnp.minimum(2 * g + 2, nchunks - 1), 0)),
            pl.BlockSpec((L, D_MODEL), lambda g: (jnp.minimum(2 * g + 3, nchunks - 1), 0)),
            const((1, D_MODEL)),
            pl.BlockSpec(w_main.shape, lambda g: (0, 0, 0), pipeline_mode=pl.Buffered(1)),
            const((GATE_ROWS, D_MODEL)),
            const((ML_CONV, 2 * ML_QK)),
            const((1, 2 * ML_QK)),
            const((GATE_ROWS, LANES)),
            const((1, ML_WIDTH)),
            const((2, HG_WIDTH)),
            const((1, HG_WIDTH)),
        ],
        out_specs=pl.BlockSpec((2 * L, ML_WIDTH + HG_WIDTH), lambda g: (g, 0)),
        out_shape=jax.ShapeDtypeStruct((t, ML_WIDTH + HG_WIDTH), BF16),
        scratch_shapes=[
            pltpu.VMEM((L, MAIN_COLS), F32),
            pltpu.VMEM((L, MAIN_COLS), F32),
            pltpu.VMEM((GATE_ROWS, L), F32),
            pltpu.VMEM((GATE_ROWS, L), F32),
            pltpu.VMEM((L, D_MODEL), BF16),
            pltpu.VMEM((L, D_MODEL), BF16),
            pltpu.VMEM((L + SUBLANES, 2 * ML_QK), F32),
            pltpu.VMEM((ML_HEADS, ML_DQK, ML_DV), F32),
            pltpu.VMEM((ML_HEADS, SUBLANES, ML_DQK), F32),
            pltpu.VMEM((SUBLANES, LANES), F32),
            pltpu.VMEM((HG_HEADS // 2, HG_DV, 2 * HG_E), F32),
            pltpu.VMEM((L, 2 * L), jnp.int32),
        ],
        compiler_params=pltpu.CompilerParams(
            dimension_semantics=("arbitrary",),
            vmem_limit_bytes=V7X_VMEM_LIMIT),
        name="mixers",
    )(x2, x2, x2, norm_w, w_main, w_gate, conv_w, conv_b, gbias, ml_norm_w, hg_lb, hg_norm_w)


def _out_kernel(x_ref, y_ref, p_ref, wout_ref, wpg_ref, wpe_ref, pnw_ref, fnw_ref, o_ref):
    half = x_ref.shape[0] // OUT_SPLIT
    parts = [slice(i * half, (i + 1) * half) for i in range(OUT_SPLIT)]
    hs = [x_ref[rows, :] + _dot(y_ref[rows, :], wout_ref[...]) for rows in parts]
    for rows, h in zip(parts, hs):
        hn = h * _rms_scale(h) * pnw_ref[...]
        gate = _sigmoid(_dot(hn.astype(BF16), wpg_ref[...]))
        emb = _dot(p_ref[rows, :].astype(BF16), wpe_ref[...])
        h = h + gate * emb
        o_ref[rows, :] = h * _rms_scale(h) * fnw_ref[...]


def _out_stage(x2, y, p2, w_out, w_pg, w_pe, pe_norm_w, final_norm_w):
    t = x2.shape[0]
    tm = min(TM_OUT, t)
    const = lambda shape: pl.BlockSpec(shape, lambda i: (0, 0), pipeline_mode=pl.Buffered(1))
    return pl.pallas_call(
        _out_kernel,
        grid=(t // tm,),
        in_specs=[
            pl.BlockSpec((tm, D_MODEL), lambda i: (i, 0)),
            pl.BlockSpec((tm, D_MODEL), lambda i: (i, 0)),
            pl.BlockSpec((tm, PE_DIM), lambda i: (i, 0)),
            const(w_out.shape),
            const(w_pg.shape),
            const(w_pe.shape),
            const((1, D_MODEL)),
            const((1, D_MODEL)),
        ],
        out_specs=pl.BlockSpec((tm, D_MODEL), lambda i: (i, 0)),
        out_shape=jax.ShapeDtypeStruct((t, D_MODEL), F32),
        compiler_params=pltpu.CompilerParams(
            dimension_semantics=("arbitrary",),
            vmem_limit_bytes=V7X_VMEM_LIMIT),
        name="out_stage",
    )(x2, y, p2, w_out, w_pg, w_pe, pe_norm_w, final_norm_w)


def kernel(x, p, norm_w, w_in, conv_w, conv_b, ml_b_i, ml_b_f, ml_norm_w, hg_lb, hg_norm_w,
           w_out, pe_norm_w, w_pg, w_pe, final_norm_w):
    batch, seq, _ = x.shape
    t = batch * seq
    chunk = min(CHUNK, seq)
    x2 = x.reshape(t, D_MODEL).astype(F32)
    p2 = p.reshape(-1, PE_DIM).astype(F32)

    w = w_in[0]
    n_gate = 2 * ML_HEADS
    gate_lo = 2 * ML_QK + 3 * ML_WIDTH
    w_main = jnp.concatenate([w[:, :gate_lo], w[:, gate_lo + n_gate:]], axis=1).astype(BF16)
    w_main = _col_tiles(w_main, TN_IN)
    w_gate = jnp.pad(w[:, gate_lo:gate_lo + n_gate].T, ((0, GATE_ROWS - n_gate), (0, 0))).astype(BF16)
    gbias = jnp.pad(jnp.concatenate([ml_b_i[0], ml_b_f[0]]).astype(F32), (0, GATE_ROWS - n_gate))
    gbias = jnp.broadcast_to(gbias[:, None], (GATE_ROWS, LANES))
    nw = norm_w[0].reshape(1, D_MODEL).astype(F32)

    y = _mixers(x2, nw, w_main, w_gate,
                conv_w[0].astype(F32), conv_b[0].reshape(1, -1).astype(F32), gbias,
                ml_norm_w[0].reshape(1, -1).astype(F32), hg_lb.astype(F32),
                hg_norm_w[0].reshape(1, -1).astype(F32), chunk, seq // chunk)
    out = _out_stage(x2, y, p2, w_out[0].astype(BF16), w_pg[0].astype(BF16), w_pe[0].astype(BF16),
                     pe_norm_w[0].reshape(1, -1).astype(F32),
                     final_norm_w.reshape(1, -1).astype(F32))
    return out.reshape(batch, seq, D_MODEL).astype(x.dtype)
```

```python
import functools
import math

import jax
import jax.numpy as jnp
from jax import lax
from jax.experimental import pallas as pl
from jax.experimental.pallas import tpu as pltpu

F32 = jnp.float32
BF16 = jnp.bfloat16

D_MODEL = 2048
PE_DIM = 256
ML_HEADS = 4
ML_DQK = 128
ML_DV = 256
ML_QK = ML_HEADS * ML_DQK
ML_WIDTH = ML_HEADS * ML_DV
ML_CONV = 4
HG_HEADS = 8
HG_E = 128
HG_DV = 128
HG_WIDTH = HG_HEADS * HG_DV
EPS = 1e-6

MAIN_COLS = 8192
COL_QK = 0
COL_V = 1024
COL_O = 2048
COL_Z = 3072
COL_HQ = 4096
COL_HF = 5120
COL_HI = 6144
COL_HG = 7168
GATE_ROWS = 16

LANES = 128
SUBLANES = 8
NEG_BIG = -1e30
V7X_VMEM_LIMIT = 58 * 1024 * 1024

CHUNK = 128
TN_IN = 256
TM_OUT = 512
OUT_SPLIT = 2


def _dot(a, b):
    return jnp.dot(a, b, preferred_element_type=F32)


def _dot_nt(a, b):
    return lax.dot_general(a, b, (((1,), (1,)), ((), ())), preferred_element_type=F32)


def _dot_tn(a, b):
    return lax.dot_general(a, b, (((0,), (0,)), ((), ())), preferred_element_type=F32)


def _col_tiles(w, tn):
    k, n = w.shape
    return w.reshape(k, n // tn, tn).transpose(1, 0, 2)


def _rms_scale(x):
    return lax.rsqrt(jnp.mean(x * x, axis=-1, keepdims=True) + EPS)


def _scan_lanes(x, op, fill):
    n = x.shape[1]
    lane = lax.broadcasted_iota(jnp.int32, x.shape, 1)
    s = 1
    while s < n:
        shifted = pltpu.roll(x, s, axis=1)
        x = op(x, jnp.where(lane >= s, shifted, fill))
        s *= 2
    return x


def _sigmoid(x):
    return 1.0 / (1.0 + jnp.exp(-x))


def _silu(x):
    return x * _sigmoid(x)


def _normalise(x, nw_ref, u_ref):
    u_ref[...] = (x * _rms_scale(x) * nw_ref[...]).astype(BF16)


def _cols(slab_ref, c0, c1):
    tn = slab_ref.shape[2]
    parts = []
    while c0 < c1:
        t, off = divmod(c0, tn)
        n = min(c1 - c0, tn - off)
        parts.append(slab_ref[t, :, off:off + n])
        c0 += n
    return parts[0] if len(parts) == 1 else jnp.concatenate(parts, axis=1)


def _projection_tasks(u_ref, w_ref, wg_ref, proj_dst, gate_dst):
    n_tiles = w_ref.shape[0]

    def gates():
        gate_dst[...] = _dot_nt(wg_ref[...], u_ref[...])

    def tile(j):
        proj_dst[j] = _dot(u_ref[...], w_ref[j])

    return [gates] + [functools.partial(tile, j) for j in range(n_tiles)]


class _Filler:
    def __init__(self, tasks, slots):
        self._tasks = list(tasks)
        self._total = len(self._tasks)
        self._slots = slots
        self._used = 0

    def __call__(self, n=1):
        self._used += n
        due = -(-self._total * self._used // self._slots)
        while self._total - len(self._tasks) < due and self._tasks:
            self._tasks.pop(0)()

    def flush(self):
        while self._tasks:
            self._tasks.pop(0)()


def _mlstm_gates(gate_ref, gbias_ref, m_ref, L):
    i8 = gate_ref[0:SUBLANES, :] + gbias_ref[0:SUBLANES, 0:1]
    f8 = pltpu.roll(i8, ML_HEADS, axis=0)
    lf8 = jnp.minimum(f8, 0.0) - jnp.log1p(jnp.exp(-jnp.abs(f8)))
    b8 = _scan_lanes(lf8, jnp.add, 0.0)
    a8 = i8 - b8
    cm8 = _scan_lanes(a8, jnp.maximum, NEG_BIG)
    mprev8 = m_ref[:, 0:1]
    mm8 = jnp.maximum(mprev8, cm8)
    mlast8 = jnp.max(mm8, axis=1, keepdims=True)
    g8 = jnp.sum(lf8, axis=1, keepdims=True)
    m_ref[...] = jnp.broadcast_to(g8 + mlast8, m_ref.shape)
    table = jnp.concatenate(
        [mm8, b8, a8,
         jnp.broadcast_to(mlast8, (SUBLANES, L)),
         jnp.broadcast_to(mprev8, (SUBLANES, L)),
         jnp.zeros((LANES - 5 * SUBLANES, L), F32)], axis=0)
    return a8, table.T


def _mlstm_head(h, q_all, k_all, qk_h, a8, ct, causal, proj_ref, mlnw_ref, c_ref, n_ref, y_ref, rows,
                fill):
    mm_c = ct[:, h:h + 1]
    b_c = ct[:, 8 + h:9 + h]
    a_c = ct[:, 16 + h:17 + h]
    mlast_c = ct[:, 24 + h:25 + h]
    mprev_c = ct[:, 32 + h:33 + h]
    a_r = a8[h:h + 1, :]

    qh = q_all[:, h * ML_DQK:(h + 1) * ML_DQK]
    kh = k_all[:, h * ML_DQK:(h + 1) * ML_DQK]
    vb = _cols(proj_ref, COL_V + h * ML_DV, COL_V + (h + 1) * ML_DV).astype(BF16)

    decay = jnp.exp(jnp.where(causal, a_r - mm_c, NEG_BIG))
    scores = qk_h * decay
    w_inter = jnp.exp(mprev_c - mm_c)
    c_old = c_ref[h]
    n_old = n_ref[h]
    num = _dot(scores.astype(BF16), vb) + w_inter * _dot(qh.astype(BF16), c_old.astype(BF16))
    den = (jnp.sum(scores, axis=-1, keepdims=True)
           + w_inter * jnp.sum(qh * n_old[0:1, :], axis=-1, keepdims=True))
    hh = num / jnp.maximum(jnp.abs(den), jnp.exp(-(b_c + mm_c)))
    fill()

    wa_c = jnp.exp(a_c - mlast_c)
    ws_c = jnp.exp(mprev_c - mlast_c)
    kw = kh * wa_c
    c_ref[h] = ws_c[0:ML_DQK, :] * c_old + _dot_tn(kw.astype(BF16), vb)
    n_ref[h] = ws_c[0:SUBLANES, :] * n_old + jnp.sum(kw, axis=0, keepdims=True)

    hn = hh * _rms_scale(hh) * mlnw_ref[:, h * ML_DV:(h + 1) * ML_DV]
    o_gate = _sigmoid(_cols(proj_ref, COL_O + h * ML_DV, COL_O + (h + 1) * ML_DV))
    z = _cols(proj_ref, COL_Z + h * ML_DV, COL_Z + (h + 1) * ML_DV)
    y_ref[rows, h * ML_DV:(h + 1) * ML_DV] = (o_gate * hn * _silu(z)).astype(y_ref.dtype)


def _split_halves(a, m):
    n, w = a.shape
    a4 = a.reshape(n // (2 * m), 2, m, w)
    return a4[:, 0], a4[:, 1]


def _merge_halves(lo, hi):
    nb, m, w = lo.shape
    return jnp.stack([lo, hi], axis=1).reshape(2 * nb * m, w)


def _block_diag(a):
    z = jnp.zeros((a.shape[0], LANES), a.dtype)
    return jnp.concatenate([jnp.concatenate([a[:, :LANES], z], axis=1),
                            jnp.concatenate([z, a[:, LANES:]], axis=1)], axis=0)


def _hgrn2_pair(hp, lb_all, proj_ref, hgnw_ref, st_ref, lv_ref, y_ref, rows, fill, fill_levels):
    L = proj_ref.shape[1]
    nlev = int(math.log2(L))
    width = 2 * HG_E
    sl = slice(hp * width, (hp + 1) * width)
    lb = lb_all[:, sl]
    q = _cols(proj_ref, COL_HQ + hp * width, COL_HQ + (hp + 1) * width)
    v = _cols(proj_ref, COL_HI + hp * width, COL_HI + (hp + 1) * width)
    f_pre = _cols(proj_ref, COL_HF + hp * width, COL_HF + (hp + 1) * width)
    kk = (1.0 - lb) * _sigmoid(-f_pre)
    c = jnp.log(lb + (1.0 - lb) * _sigmoid(f_pre))

    att = None
    q8 = q.reshape(L // SUBLANES, SUBLANES, width)
    kk8 = kk.reshape(L // SUBLANES, SUBLANES, width)
    sub = lax.broadcasted_iota(jnp.int32, q8.shape, 1)
    for lev in range(nlev):
        m = 1 << lev
        if m < SUBLANES:
            c8 = c.reshape(L // SUBLANES, SUBLANES, width)
            upper = (sub & m) != 0
            tot = c8
            s = 1
            while s < m:
                tot = jnp.where((sub & s) != 0, tot, pltpu.roll(tot, SUBLANES - s, axis=1))
                s *= 2
            e = jnp.exp(jnp.where(upper, c8, tot - c8))
            zz = (jnp.where(upper, q8, kk8) * e).reshape(L, width).astype(BF16)
            p = jnp.where(lv_ref[...] == lev, _dot_nt(zz, _block_diag(zz)), 0.0)
            att = p if att is None else att + p
            c = (c8 + jnp.where(upper, pltpu.roll(tot, m, axis=1), 0.0)).reshape(L, width)
        else:
            c_lo, c_hi = _split_halves(c, m)
            tot_lo = jnp.broadcast_to(c_lo[:, m - 1:m, :], c_lo.shape)
            z_lo = _split_halves(kk, m)[0] * jnp.exp(tot_lo - c_lo)
            z_hi = _split_halves(q, m)[1] * jnp.exp(c_hi)
            zz = _merge_halves(z_lo, z_hi).astype(BF16)
            p = _dot_nt(z_hi.reshape(L // 2, width).astype(BF16), _block_diag(zz))
            lv_hi = _split_halves(lv_ref[...], m)[1].reshape(L // 2, 2 * L)
            p = jnp.where(lv_hi == lev, p, 0.0).reshape(L // (2 * m), m, 2 * L)
            att_lo, att_hi = _split_halves(att, m)
            att = _merge_halves(att_lo, att_hi + p)
            c = _merge_halves(c_lo, c_hi + tot_lo)
        if lev in fill_levels:
            fill()

    tot = jnp.broadcast_to(c[L - 1:L, :], c.shape)
    st_old = st_ref[hp]
    vb = v.astype(BF16)
    qk_diag = q * kk
    o = (_dot(att.astype(BF16), _block_diag(vb))
         + _dot_nt((q * jnp.exp(c)).astype(BF16), _block_diag(st_old.astype(BF16))))
    ke = (kk * jnp.exp(tot - c)).astype(BF16)
    upd = jnp.concatenate([_dot_tn(vb[:, :HG_DV], ke[:, :HG_E]),
                           _dot_tn(vb[:, HG_DV:], ke[:, HG_E:])], axis=1)
    st_ref[hp] = st_old * jnp.exp(tot[0:HG_DV, :]) + upd

    gz = _cols(proj_ref, COL_HG + hp * width, COL_HG + (hp + 1) * width)
    gate = _silu(gz)
    nw = hgnw_ref[:, sl]
    for i in range(2):
        hs = slice(i * HG_DV, (i + 1) * HG_DV)
        oi = o[:, hs] + jnp.sum(qk_diag[:, hs], axis=-1, keepdims=True) * v[:, hs]
        on = oi * _rms_scale(oi) * nw[:, hs]
        col = ML_WIDTH + hp * width + i * HG_DV
        y_ref[rows, col:col + HG_DV] = (on * gate[:, hs]).astype(y_ref.dtype)


def _mix(proj_ref, gate_ref, convw_ref, convb_ref, gbias_ref, mlnw_ref, hglb_ref, hgnw_ref,
         y_ref, r0, qkext_ref, c_ref, n_ref, m_ref, st_ref, lv_ref, tasks, late_task):
    L = proj_ref.shape[1]
    rows = slice(r0, r0 + L)
    fill = _Filler(tasks, slots=6 + 5 * (HG_HEADS // 2) + 2 * ML_HEADS)

    fill(2)
    a8, ct = _mlstm_gates(gate_ref, gbias_ref, m_ref, L)
    fill(2)

    qkext_ref[SUBLANES:SUBLANES + L, :] = _cols(proj_ref, COL_QK, COL_QK + 2 * ML_QK)
    conv = convb_ref[...]
    for k in range(ML_CONV):
        off = SUBLANES - (ML_CONV - 1) + k
        conv = conv + qkext_ref[off:off + L, :] * convw_ref[k:k + 1, :]
    qkext_ref[0:SUBLANES, :] = qkext_ref[L:L + SUBLANES, :]
    qk = _silu(conv)
    q_all = qk[:, :ML_QK]
    k_all = qk[:, ML_QK:] * (ML_DQK ** -0.5)
    fill(2)

    lbl = hglb_ref[...]
    lmax = jnp.max(lbl, axis=0, keepdims=True)
    lexp = jnp.exp(lbl - lmax)
    lb_all = lexp[0:1, :] / jnp.sum(lexp, axis=0, keepdims=True)
    for hp in range(HG_HEADS // 2):
        _hgrn2_pair(hp, lb_all, proj_ref, hgnw_ref, st_ref, lv_ref, y_ref, rows, fill,
                    fill_levels=(0, 1, 3, 4, 6))
        if hp == HG_HEADS // 4 - 1:
            late_task()

    rowi = lax.broadcasted_iota(jnp.int32, (L, L), 0)
    coli = lax.broadcasted_iota(jnp.int32, (L, L), 1)
    causal = coli <= rowi
    for h in range(ML_HEADS):
        if h % 2 == 0:
            pair = slice(h * ML_DQK, (h + 2) * ML_DQK)
            qk_pair = _dot_nt(q_all[:, pair].astype(BF16), _block_diag(k_all[:, pair].astype(BF16)))
        qk_h = qk_pair[:, (h % 2) * L:(h % 2 + 1) * L]
        _mlstm_head(h, q_all, k_all, qk_h, a8, ct, causal, proj_ref, mlnw_ref, c_ref, n_ref, y_ref,
                    rows, fill)
        fill()
    fill.flush()


def _mixer_kernel(steps_per_seq,
                  x0_ref, xa_ref, xb_ref, nw_ref, w_ref, wg_ref,
                  convw_ref, convb_ref, gbias_ref, mlnw_ref, hglb_ref, hgnw_ref,
                  y_ref,
                  proj_a, proj_b, gate_a, gate_b, u_a, u_b,
                  qkext_ref, c_ref, n_ref, m_ref, st_ref, lv_ref):
    L = proj_a.shape[1]
    g = pl.program_id(0)

    @pl.when(g == 0)
    def _():
        _normalise(x0_ref[0:L, :], nw_ref, u_a)
        gate_a[...] = _dot_nt(wg_ref[...], u_a[...])

        def project_tile(j, carry):
            proj_a[j] = _dot(u_a[...], w_ref[j])
            return carry

        lax.fori_loop(0, w_ref.shape[0], project_tile, 0)
        _normalise(x0_ref[L:2 * L, :], nw_ref, u_b)
        r = lax.broadcasted_iota(jnp.int32, (L, L), 0)
        c = lax.broadcasted_iota(jnp.int32, (L, L), 1)
        lv = 31 - lax.clz(r ^ c)
        lv = jnp.where(r > c, lv, -1)
        lv_ref[...] = jnp.concatenate([lv, lv], axis=1)

    @pl.when(g % steps_per_seq == 0)
    def _():
        qkext_ref[0:SUBLANES, :] = jnp.zeros((SUBLANES, 2 * ML_QK), F32)
        c_ref[...] = jnp.zeros(c_ref.shape, F32)
        n_ref[...] = jnp.zeros(n_ref.shape, F32)
        m_ref[...] = jnp.zeros(m_ref.shape, F32)
        st_ref[...] = jnp.zeros(st_ref.shape, F32)

    params = (convw_ref, convb_ref, gbias_ref, mlnw_ref, hglb_ref, hgnw_ref)
    state = (qkext_ref, c_ref, n_ref, m_ref, st_ref, lv_ref)
    _mix(proj_a, gate_a, *params, y_ref, 0, *state,
         _projection_tasks(u_b, w_ref, wg_ref, proj_b, gate_b),
         lambda: _normalise(xa_ref[...], nw_ref, u_a))
    _mix(proj_b, gate_b, *params, y_ref, L, *state,
         _projection_tasks(u_a, w_ref, wg_ref, proj_a, gate_a),
         lambda: _normalise(xb_ref[...], nw_ref, u_b))


def _mixers(x2, norm_w, w_main, w_gate, conv_w, conv_b, gbias, ml_norm_w, hg_lb,
            hg_norm_w, chunk, chunks_per_seq):
    t = x2.shape[0]
    L = chunk
    nchunks = t // L
    steps = nchunks // 2
    assert chunks_per_seq % 2 == 0
    const = lambda shape: pl.BlockSpec(shape, lambda g: (0, 0), pipeline_mode=pl.Buffered(1))
    return pl.pallas_call(
        functools.partial(_mixer_kernel, chunks_per_seq // 2),
        grid=(steps,),
        in_specs=[
            const((2 * L, D_MODEL)),
            pl.BlockSpec((L, D_MODEL), lambda g: (jnp.minimum(2 * g + 2, nchunks - 1), 0)),
            pl.BlockSpec((L, D_MODEL), lambda g: (jnp.minimum(2 * g + 3, nchunks - 1), 0)),
            const((1, D_MODEL)),
            pl.BlockSpec(w_main.shape, lambda g: (0, 0, 0), pipeline_mode=pl.Buffered(1)),
            const((GATE_ROWS, D_MODEL)),
            const((ML_CONV, 2 * ML_QK)),
            const((1, 2 * ML_QK)),
            const((GATE_ROWS, LANES)),
            const((1, ML_WIDTH)),
            const((2, HG_WIDTH)),
            const((1, HG_WIDTH)),
        ],
        out_specs=pl.BlockSpec((2 * L, ML_WIDTH + HG_WIDTH), lambda g: (g, 0)),
        out_shape=jax.ShapeDtypeStruct((t, ML_WIDTH + HG_WIDTH), BF16),
        scratch_shapes=[
            pltpu.VMEM((MAIN_COLS // TN_IN, L, TN_IN), F32),
            pltpu.VMEM((MAIN_COLS // TN_IN, L, TN_IN), F32),
            pltpu.VMEM((GATE_ROWS, L), F32),
            pltpu.VMEM((GATE_ROWS, L), F32),
            pltpu.VMEM((L, D_MODEL), BF16),
            pltpu.VMEM((L, D_MODEL), BF16),
            pltpu.VMEM((L + SUBLANES, 2 * ML_QK), F32),
            pltpu.VMEM((ML_HEADS, ML_DQK, ML_DV), F32),
            pltpu.VMEM((ML_HEADS, SUBLANES, ML_DQK), F32),
            pltpu.VMEM((SUBLANES, LANES), F32),
            pltpu.VMEM((HG_HEADS // 2, HG_DV, 2 * HG_E), F32),
            pltpu.VMEM((L, 2 * L), jnp.int32),
        ],
        compiler_params=pltpu.CompilerParams(
            dimension_semantics=("arbitrary",),
            vmem_limit_bytes=V7X_VMEM_LIMIT),
        name="mixers",
    )(x2, x2, x2, norm_w, w_main, w_gate, conv_w, conv_b, gbias, ml_norm_w, hg_lb, hg_norm_w)


def _out_kernel(x_ref, y_ref, p_ref, wout_ref, wpg_ref, wpe_ref, pnw_ref, fnw_ref, o_ref):
    half = x_ref.shape[0] // OUT_SPLIT
    parts = [slice(i * half, (i + 1) * half) for i in range(OUT_SPLIT)]
    hs = [x_ref[rows, :] + _dot(y_ref[rows, :], wout_ref[...]) for rows in parts]
    for rows, h in zip(parts, hs):
        hn = h * _rms_scale(h) * pnw_ref[...]
        gate = _sigmoid(_dot(hn.astype(BF16), wpg_ref[...]))
        emb = _dot(p_ref[rows, :].astype(BF16), wpe_ref[...])
        h = h + gate * emb
        o_ref[rows, :] = h * _rms_scale(h) * fnw_ref[...]


def _out_stage(x2, y, p2, w_out, w_pg, w_pe, pe_norm_w, final_norm_w):
    t = x2.shape[0]
    tm = min(TM_OUT, t)
    const = lambda shape: pl.BlockSpec(shape, lambda i: (0, 0), pipeline_mode=pl.Buffered(1))
    return pl.pallas_call(
        _out_kernel,
        grid=(t // tm,),
        in_specs=[
            pl.BlockSpec((tm, D_MODEL), lambda i: (i, 0)),
            pl.BlockSpec((tm, D_MODEL), lambda i: (i, 0)),
            pl.BlockSpec((tm, PE_DIM), lambda i: (i, 0)),
            const(w_out.shape),
            const(w_pg.shape),
            const(w_pe.shape),
            const((1, D_MODEL)),
            const((1, D_MODEL)),
        ],
        out_specs=pl.BlockSpec((tm, D_MODEL), lambda i: (i, 0)),
        out_shape=jax.ShapeDtypeStruct((t, D_MODEL), F32),
        compiler_params=pltpu.CompilerParams(
            dimension_semantics=("arbitrary",),
            vmem_limit_bytes=V7X_VMEM_LIMIT),
        name="out_stage",
    )(x2, y, p2, w_out, w_pg, w_pe, pe_norm_w, final_norm_w)


def kernel(x, p, norm_w, w_in, conv_w, conv_b, ml_b_i, ml_b_f, ml_norm_w, hg_lb, hg_norm_w,
           w_out, pe_norm_w, w_pg, w_pe, final_norm_w):
    batch, seq, _ = x.shape
    t = batch * seq
    chunk = min(CHUNK, seq)
    x2 = x.reshape(t, D_MODEL).astype(F32)
    p2 = p.reshape(-1, PE_DIM).astype(F32)

    w = w_in[0]
    n_gate = 2 * ML_HEADS
    gate_lo = 2 * ML_QK + 3 * ML_WIDTH
    w_main = jnp.concatenate([w[:, :gate_lo], w[:, gate_lo + n_gate:]], axis=1).astype(BF16)
    w_main = _col_tiles(w_main, TN_IN)
    w_gate = jnp.pad(w[:, gate_lo:gate_lo + n_gate].T, ((0, GATE_ROWS - n_gate), (0, 0))).astype(BF16)
    gbias = jnp.pad(jnp.concatenate([ml_b_i[0], ml_b_f[0]]).astype(F32), (0, GATE_ROWS - n_gate))
    gbias = jnp.broadcast_to(gbias[:, None], (GATE_ROWS, LANES))
    nw = norm_w[0].reshape(1, D_MODEL).astype(F32)

    y = _mixers(x2, nw, w_main, w_gate,
                conv_w[0].astype(F32), conv_b[0].reshape(1, -1).astype(F32), gbias,
                ml_norm_w[0].reshape(1, -1).astype(F32), hg_lb.astype(F32),
                hg_norm_w[0].reshape(1, -1).astype(F32), chunk, seq // chunk)
    out = _out_stage(x2, y, p2, w_out[0].astype(BF16), w_pg[0].astype(BF16), w_pe[0].astype(BF16),
                     pe_norm_w[0].reshape(1, -1).astype(F32),
                     final_norm_w.reshape(1, -1).astype(F32))
    return out.reshape(batch, seq, D_MODEL).astype(x.dtype)
```

```python
import functools
import math

import jax
import jax.numpy as jnp
from jax import lax
from jax.experimental import pallas as pl
from jax.experimental.pallas import tpu as pltpu

F32 = jnp.float32
BF16 = jnp.bfloat16

D_MODEL = 2048
PE_DIM = 256
ML_HEADS = 4
ML_DQK = 128
ML_DV = 256
ML_QK = ML_HEADS * ML_DQK
ML_WIDTH = ML_HEADS * ML_DV
ML_CONV = 4
HG_HEADS = 8
HG_E = 128
HG_DV = 128
HG_WIDTH = HG_HEADS * HG_DV
EPS = 1e-6

MAIN_COLS = 8192
COL_QK = 0
COL_V = 1024
COL_O = 2048
COL_Z = 3072
COL_HQ = 4096
COL_HF = 5120
COL_HI = 6144
COL_HG = 7168
GATE_ROWS = 16

LANES = 128
SUBLANES = 8
NEG_BIG = -1e30
V7X_VMEM_LIMIT = 58 * 1024 * 1024

CHUNK = 128
TN_IN = 256
TM_OUT = 512
OUT_SPLIT = 2
CAST_ROWS = 512


def _dot(a, b):
    return jnp.dot(a, b, preferred_element_type=F32)


def _dot_nt(a, b):
    return lax.dot_general(a, b, (((1,), (1,)), ((), ())), preferred_element_type=F32)


def _dot_tn(a, b):
    return lax.dot_general(a, b, (((0,), (0,)), ((), ())), preferred_element_type=F32)


def _col_tiles(w, tn):
    k, n = w.shape
    return w.reshape(k, n // tn, tn).transpose(1, 0, 2)


def _rms_scale(x):
    return lax.rsqrt(jnp.mean(x * x, axis=-1, keepdims=True) + EPS)


def _scan_lanes(x, op, fill):
    n = x.shape[1]
    lane = lax.broadcasted_iota(jnp.int32, x.shape, 1)
    s = 1
    while s < n:
        shifted = pltpu.roll(x, s, axis=1)
        x = op(x, jnp.where(lane >= s, shifted, fill))
        s *= 2
    return x


def _sigmoid(x):
    return 1.0 / (1.0 + jnp.exp(-x))


def _silu(x):
    return x * _sigmoid(x)


def _normalise(x, nw_ref, u_ref):
    u_ref[...] = (x * _rms_scale(x) * nw_ref[...]).astype(BF16)


def _cols(slab_ref, c0, c1):
    tn = slab_ref.shape[2]
    parts = []
    while c0 < c1:
        t, off = divmod(c0, tn)
        n = min(c1 - c0, tn - off)
        parts.append(slab_ref[t, :, off:off + n])
        c0 += n
    return parts[0] if len(parts) == 1 else jnp.concatenate(parts, axis=1)


def _projection_tasks(u_ref, w_ref, wg_ref, proj_dst, gate_dst):
    n_tiles = w_ref.shape[0]

    def gates():
        gate_dst[...] = _dot_nt(wg_ref[...], u_ref[...])

    def tile(j):
        proj_dst[j] = _dot(u_ref[...], w_ref[j])

    return [gates] + [functools.partial(tile, j) for j in range(n_tiles)]


class _Filler:
    def __init__(self, tasks, slots):
        self._tasks = list(tasks)
        self._total = len(self._tasks)
        self._slots = slots
        self._used = 0

    def __call__(self, n=1):
        self._used += n
        due = -(-self._total * self._used // self._slots)
        while self._total - len(self._tasks) < due and self._tasks:
            self._tasks.pop(0)()

    def flush(self):
        while self._tasks:
            self._tasks.pop(0)()


def _mlstm_gates(gate_ref, gbias_ref, m_ref, L):
    i8 = gate_ref[0:SUBLANES, :] + gbias_ref[0:SUBLANES, 0:1]
    f8 = pltpu.roll(i8, ML_HEADS, axis=0)
    lf8 = jnp.minimum(f8, 0.0) - jnp.log1p(jnp.exp(-jnp.abs(f8)))
    b8 = _scan_lanes(lf8, jnp.add, 0.0)
    a8 = i8 - b8
    cm8 = _scan_lanes(a8, jnp.maximum, NEG_BIG)
    mprev8 = m_ref[:, 0:1]
    mm8 = jnp.maximum(mprev8, cm8)
    mlast8 = jnp.max(mm8, axis=1, keepdims=True)
    g8 = jnp.sum(lf8, axis=1, keepdims=True)
    m_ref[...] = jnp.broadcast_to(g8 + mlast8, m_ref.shape)
    table = jnp.concatenate(
        [mm8, b8, a8,
         jnp.broadcast_to(mlast8, (SUBLANES, L)),
         jnp.broadcast_to(mprev8, (SUBLANES, L)),
         jnp.zeros((LANES - 5 * SUBLANES, L), F32)], axis=0)
    return a8, table.T


def _mlstm_head(h, q_all, k_all, qk_h, a8, ct, causal, proj_ref, mlnw_ref, c_ref, n_ref, y_ref, rows,
                fill):
    mm_c = ct[:, h:h + 1]
    b_c = ct[:, 8 + h:9 + h]
    a_c = ct[:, 16 + h:17 + h]
    mlast_c = ct[:, 24 + h:25 + h]
    mprev_c = ct[:, 32 + h:33 + h]
    a_r = a8[h:h + 1, :]

    qh = q_all[:, h * ML_DQK:(h + 1) * ML_DQK]
    kh = k_all[:, h * ML_DQK:(h + 1) * ML_DQK]
    vb = _cols(proj_ref, COL_V + h * ML_DV, COL_V + (h + 1) * ML_DV).astype(BF16)

    decay = jnp.exp(jnp.where(causal, a_r - mm_c, NEG_BIG))
    scores = qk_h * decay
    w_inter = jnp.exp(mprev_c - mm_c)
    c_old = c_ref[h]
    n_old = n_ref[h]
    num = _dot(scores.astype(BF16), vb) + w_inter * _dot(qh.astype(BF16), c_old.astype(BF16))
    den = (jnp.sum(scores, axis=-1, keepdims=True)
           + w_inter * jnp.sum(qh * n_old[0:1, :], axis=-1, keepdims=True))
    hh = num / jnp.maximum(jnp.abs(den), jnp.exp(-(b_c + mm_c)))
    fill()

    wa_c = jnp.exp(a_c - mlast_c)
    ws_c = jnp.exp(mprev_c - mlast_c)
    kw = kh * wa_c
    c_ref[h] = ws_c[0:ML_DQK, :] * c_old + _dot_tn(kw.astype(BF16), vb)
    n_ref[h] = ws_c[0:SUBLANES, :] * n_old + jnp.sum(kw, axis=0, keepdims=True)

    hn = hh * _rms_scale(hh) * mlnw_ref[:, h * ML_DV:(h + 1) * ML_DV]
    o_gate = _sigmoid(_cols(proj_ref, COL_O + h * ML_DV, COL_O + (h + 1) * ML_DV))
    z = _cols(proj_ref, COL_Z + h * ML_DV, COL_Z + (h + 1) * ML_DV)
    y_ref[rows, h * ML_DV:(h + 1) * ML_DV] = (o_gate * hn * _silu(z)).astype(y_ref.dtype)


def _split_halves(a, m):
    n, w = a.shape
    a4 = a.reshape(n // (2 * m), 2, m, w)
    return a4[:, 0], a4[:, 1]


def _merge_halves(lo, hi):
    nb, m, w = lo.shape
    return jnp.stack([lo, hi], axis=1).reshape(2 * nb * m, w)


def _block_diag(a):
    z = jnp.zeros((a.shape[0], LANES), a.dtype)
    return jnp.concatenate([jnp.concatenate([a[:, :LANES], z], axis=1),
                            jnp.concatenate([z, a[:, LANES:]], axis=1)], axis=0)


def _hgrn2_pair(hp, lb_all, proj_ref, hgnw_ref, st_ref, lv_ref, y_ref, rows, fill, fill_levels):
    L = proj_ref.shape[1]
    nlev = int(math.log2(L))
    width = 2 * HG_E
    sl = slice(hp * width, (hp + 1) * width)
    lb = lb_all[:, sl]
    q = _cols(proj_ref, COL_HQ + hp * width, COL_HQ + (hp + 1) * width)
    v = _cols(proj_ref, COL_HI + hp * width, COL_HI + (hp + 1) * width)
    f_pre = _cols(proj_ref, COL_HF + hp * width, COL_HF + (hp + 1) * width)
    kk = (1.0 - lb) * _sigmoid(-f_pre)
    c = jnp.log(lb + (1.0 - lb) * _sigmoid(f_pre))

    q8 = q.reshape(L // SUBLANES, SUBLANES, width)
    kk8 = kk.reshape(L // SUBLANES, SUBLANES, width)
    sub = lax.broadcasted_iota(jnp.int32, q8.shape, 1)
    operands = []
    for lev in range(nlev):
        m = 1 << lev
        if m < SUBLANES:
            c8 = c.reshape(L // SUBLANES, SUBLANES, width)
            upper = (sub & m) != 0
            tot = c8
            s = 1
            while s < m:
                tot = jnp.where((sub & s) != 0, tot, pltpu.roll(tot, SUBLANES - s, axis=1))
                s *= 2
            e = jnp.exp(jnp.where(upper, c8, tot - c8))
            zz = (jnp.where(upper, q8, kk8) * e).reshape(L, width).astype(BF16)
            operands.append((zz, zz))
            c = (c8 + jnp.where(upper, pltpu.roll(tot, m, axis=1), 0.0)).reshape(L, width)
        else:
            c_lo, c_hi = _split_halves(c, m)
            tot_lo = jnp.broadcast_to(c_lo[:, m - 1:m, :], c_lo.shape)
            z_lo = _split_halves(kk, m)[0] * jnp.exp(tot_lo - c_lo)
            z_hi = _split_halves(q, m)[1] * jnp.exp(c_hi)
            operands.append((z_hi.reshape(L // 2, width).astype(BF16),
                             _merge_halves(z_lo, z_hi).astype(BF16)))
            c = _merge_halves(c_lo, c_hi + tot_lo)
        if lev in fill_levels:
            fill()

    att = None
    for lev, (lhs, zz) in enumerate(operands):
        m = 1 << lev
        p = _dot_nt(lhs, _block_diag(zz))
        if m < SUBLANES:
            p = jnp.where(lv_ref[...] == lev, p, 0.0)
            att = p if att is None else att + p
        else:
            lv_hi = _split_halves(lv_ref[...], m)[1].reshape(L // 2, 2 * L)
            p = jnp.where(lv_hi == lev, p, 0.0).reshape(L // (2 * m), m, 2 * L)
            att_lo, att_hi = _split_halves(att, m)
            att = _merge_halves(att_lo, att_hi + p)

    tot = jnp.broadcast_to(c[L - 1:L, :], c.shape)
    st_old = st_ref[hp]
    vb = v.astype(BF16)
    qk_diag = q * kk
    o = (_dot(att.astype(BF16), _block_diag(vb))
         + _dot_nt((q * jnp.exp(c)).astype(BF16), _block_diag(st_old.astype(BF16))))
    ke = (kk * jnp.exp(tot - c)).astype(BF16)
    upd = jnp.concatenate([_dot_tn(vb[:, :HG_DV], ke[:, :HG_E]),
                           _dot_tn(vb[:, HG_DV:], ke[:, HG_E:])], axis=1)
    st_ref[hp] = st_old * jnp.exp(tot[0:HG_DV, :]) + upd

    gz = _cols(proj_ref, COL_HG + hp * width, COL_HG + (hp + 1) * width)
    gate = _silu(gz)
    nw = hgnw_ref[:, sl]
    for i in range(2):
        hs = slice(i * HG_DV, (i + 1) * HG_DV)
        oi = o[:, hs] + jnp.sum(qk_diag[:, hs], axis=-1, keepdims=True) * v[:, hs]
        on = oi * _rms_scale(oi) * nw[:, hs]
        col = ML_WIDTH + hp * width + i * HG_DV
        y_ref[rows, col:col + HG_DV] = (on * gate[:, hs]).astype(y_ref.dtype)


def _mix(proj_ref, gate_ref, convw_ref, convb_ref, gbias_ref, mlnw_ref, hglb_ref, hgnw_ref,
         y_ref, r0, qkext_ref, c_ref, n_ref, m_ref, st_ref, lv_ref, tasks, late_task):
    L = proj_ref.shape[1]
    rows = slice(r0, r0 + L)
    fill = _Filler(tasks, slots=6 + 5 * (HG_HEADS // 2) + 2 * ML_HEADS)

    fill(2)
    a8, ct = _mlstm_gates(gate_ref, gbias_ref, m_ref, L)
    fill(2)

    qkext_ref[SUBLANES:SUBLANES + L, :] = _cols(proj_ref, COL_QK, COL_QK + 2 * ML_QK)
    conv = convb_ref[...]
    for k in range(ML_CONV):
        off = SUBLANES - (ML_CONV - 1) + k
        conv = conv + qkext_ref[off:off + L, :] * convw_ref[k:k + 1, :]
    qkext_ref[0:SUBLANES, :] = qkext_ref[L:L + SUBLANES, :]
    qk = _silu(conv)
    q_all = qk[:, :ML_QK]
    k_all = qk[:, ML_QK:] * (ML_DQK ** -0.5)
    fill(2)

    lbl = hglb_ref[...]
    lmax = jnp.max(lbl, axis=0, keepdims=True)
    lexp = jnp.exp(lbl - lmax)
    lb_all = lexp[0:1, :] / jnp.sum(lexp, axis=0, keepdims=True)
    for hp in range(HG_HEADS // 2):
        _hgrn2_pair(hp, lb_all, proj_ref, hgnw_ref, st_ref, lv_ref, y_ref, rows, fill,
                    fill_levels=(0, 1, 3, 4, 6))
        if hp == HG_HEADS // 4 - 1:
            late_task()

    rowi = lax.broadcasted_iota(jnp.int32, (L, L), 0)
    coli = lax.broadcasted_iota(jnp.int32, (L, L), 1)
    causal = coli <= rowi
    for h in range(ML_HEADS):
        if h % 2 == 0:
            pair = slice(h * ML_DQK, (h + 2) * ML_DQK)
            qk_pair = _dot_nt(q_all[:, pair].astype(BF16), _block_diag(k_all[:, pair].astype(BF16)))
        qk_h = qk_pair[:, (h % 2) * L:(h % 2 + 1) * L]
        _mlstm_head(h, q_all, k_all, qk_h, a8, ct, causal, proj_ref, mlnw_ref, c_ref, n_ref, y_ref,
                    rows, fill)
        fill()
    fill.flush()


def _mixer_kernel(steps_per_seq,
                  x0_ref, xa_ref, xb_ref, nw_ref, w_ref, wg_ref,
                  convw_ref, convb_ref, gbias_ref, mlnw_ref, hglb_ref, hgnw_ref,
                  y_ref,
                  proj_a, proj_b, gate_a, gate_b, u_a, u_b,
                  qkext_ref, c_ref, n_ref, m_ref, st_ref, lv_ref):
    L = proj_a.shape[1]
    g = pl.program_id(0)

    @pl.when(g == 0)
    def _():
        _normalise(x0_ref[0:L, :], nw_ref, u_a)
        gate_a[...] = _dot_nt(wg_ref[...], u_a[...])

        def project_tile(j, carry):
            proj_a[j] = _dot(u_a[...], w_ref[j])
            return carry

        lax.fori_loop(0, w_ref.shape[0], project_tile, 0)
        _normalise(x0_ref[L:2 * L, :], nw_ref, u_b)
        r = lax.broadcasted_iota(jnp.int32, (L, L), 0)
        c = lax.broadcasted_iota(jnp.int32, (L, L), 1)
        lv = 31 - lax.clz(r ^ c)
        lv = jnp.where(r > c, lv, -1)
        lv_ref[...] = jnp.concatenate([lv, lv], axis=1)

    @pl.when(g % steps_per_seq == 0)
    def _():
        qkext_ref[0:SUBLANES, :] = jnp.zeros((SUBLANES, 2 * ML_QK), F32)
        c_ref[...] = jnp.zeros(c_ref.shape, F32)
        n_ref[...] = jnp.zeros(n_ref.shape, F32)
        m_ref[...] = jnp.zeros(m_ref.shape, F32)
        st_ref[...] = jnp.zeros(st_ref.shape, F32)

    params = (convw_ref, convb_ref, gbias_ref, mlnw_ref, hglb_ref, hgnw_ref)
    state = (qkext_ref, c_ref, n_ref, m_ref, st_ref, lv_ref)
    _mix(proj_a, gate_a, *params, y_ref, 0, *state,
         _projection_tasks(u_b, w_ref, wg_ref, proj_b, gate_b),
         lambda: _normalise(xa_ref[...], nw_ref, u_a))
    _mix(proj_b, gate_b, *params, y_ref, L, *state,
         _projection_tasks(u_a, w_ref, wg_ref, proj_a, gate_a),
         lambda: _normalise(xb_ref[...], nw_ref, u_b))


def _mixers(x2, norm_w, w_main, w_gate, conv_w, conv_b, gbias, ml_norm_w, hg_lb,
            hg_norm_w, chunk, chunks_per_seq):
    t = x2.shape[0]
    L = chunk
    nchunks = t // L
    steps = nchunks // 2
    assert chunks_per_seq % 2 == 0
    const = lambda shape: pl.BlockSpec(shape, lambda g: (0, 0), pipeline_mode=pl.Buffered(1))
    return pl.pallas_call(
        functools.partial(_mixer_kernel, chunks_per_seq // 2),
        grid=(steps,),
        in_specs=[
            const((2 * L, D_MODEL)),
            pl.BlockSpec((L, D_MODEL), lambda g: (jnp.minimum(2 * g + 2, nchunks - 1), 0)),
            pl.BlockSpec((L, D_MODEL), lambda g: (jnp.minimum(2 * g + 3, nchunks - 1), 0)),
            const((1, D_MODEL)),
            pl.BlockSpec(w_main.shape, lambda g: (0, 0, 0), pipeline_mode=pl.Buffered(1)),
            const((GATE_ROWS, D_MODEL)),
            const((ML_CONV, 2 * ML_QK)),
            const((1, 2 * ML_QK)),
            const((GATE_ROWS, LANES)),
            const((1, ML_WIDTH)),
            const((2, HG_WIDTH)),
            const((1, HG_WIDTH)),
        ],
        out_specs=pl.BlockSpec((2 * L, ML_WIDTH + HG_WIDTH), lambda g: (g, 0)),
        out_shape=jax.ShapeDtypeStruct((t, ML_WIDTH + HG_WIDTH), BF16),
        scratch_shapes=[
            pltpu.VMEM((MAIN_COLS // TN_IN, L, TN_IN), F32),
            pltpu.VMEM((MAIN_COLS // TN_IN, L, TN_IN), F32),
            pltpu.VMEM((GATE_ROWS, L), F32),
            pltpu.VMEM((GATE_ROWS, L), F32),
            pltpu.VMEM((L, D_MODEL), BF16),
            pltpu.VMEM((L, D_MODEL), BF16),
            pltpu.VMEM((L + SUBLANES, 2 * ML_QK), F32),
            pltpu.VMEM((ML_HEADS, ML_DQK, ML_DV), F32),
            pltpu.VMEM((ML_HEADS, SUBLANES, ML_DQK), F32),
            pltpu.VMEM((SUBLANES, LANES), F32),
            pltpu.VMEM((HG_HEADS // 2, HG_DV, 2 * HG_E), F32),
            pltpu.VMEM((L, 2 * L), jnp.int32),
        ],
        compiler_params=pltpu.CompilerParams(
            dimension_semantics=("arbitrary",),
            vmem_limit_bytes=V7X_VMEM_LIMIT),
        name="mixers",
    )(x2, x2, x2, norm_w, w_main, w_gate, conv_w, conv_b, gbias, ml_norm_w, hg_lb, hg_norm_w)


def _out_kernel(x_ref, y_ref, p_ref, wout_ref, wpg_ref, wpe_ref, pnw_ref, fnw_ref, o_ref):
    half = x_ref.shape[0] // OUT_SPLIT
    parts = [slice(i * half, (i + 1) * half) for i in range(OUT_SPLIT)]
    hs = [x_ref[rows, :] + _dot(y_ref[rows, :], wout_ref[...]) for rows in parts]
    for rows, h in zip(parts, hs):
        hn = h * _rms_scale(h) * pnw_ref[...]
        gate = _sigmoid(_dot(hn.astype(BF16), wpg_ref[...]))
        emb = _dot(p_ref[rows, :].astype(BF16), wpe_ref[...])
        h = h + gate * emb
        o_ref[rows, :] = h * _rms_scale(h) * fnw_ref[...]


def _cast_kernel(w_ref, o_ref):
    o_ref[...] = w_ref[...].astype(o_ref.dtype)


def _to_bf16(w):
    rows, cols = w.shape
    tr = min(CAST_ROWS, rows)
    return pl.pallas_call(
        _cast_kernel,
        grid=(rows // tr,),
        in_specs=[pl.BlockSpec((tr, cols), lambda i: (i, 0))],
        out_specs=pl.BlockSpec((tr, cols), lambda i: (i, 0)),
        out_shape=jax.ShapeDtypeStruct((rows, cols), BF16),
        compiler_params=pltpu.CompilerParams(dimension_semantics=("arbitrary",)),
        name="cast_bf16",
    )(w)


def _out_stage(x2, y, p2, w_out, w_pg, w_pe, pe_norm_w, final_norm_w):
    t = x2.shape[0]
    tm = min(TM_OUT, t)
    const = lambda shape: pl.BlockSpec(shape, lambda i: (0, 0), pipeline_mode=pl.Buffered(1))
    return pl.pallas_call(
        _out_kernel,
        grid=(t // tm,),
        in_specs=[
            pl.BlockSpec((tm, D_MODEL), lambda i: (i, 0)),
            pl.BlockSpec((tm, D_MODEL), lambda i: (i, 0)),
            pl.BlockSpec((tm, PE_DIM), lambda i: (i, 0)),
            const(w_out.shape),
            const(w_pg.shape),
            const(w_pe.shape),
            const((1, D_MODEL)),
            const((1, D_MODEL)),
        ],
        out_specs=pl.BlockSpec((tm, D_MODEL), lambda i: (i, 0)),
        out_shape=jax.ShapeDtypeStruct((t, D_MODEL), F32),
        compiler_params=pltpu.CompilerParams(
            dimension_semantics=("arbitrary",),
            vmem_limit_bytes=V7X_VMEM_LIMIT),
        name="out_stage",
    )(x2, y, p2, w_out, w_pg, w_pe, pe_norm_w, final_norm_w)


def kernel(x, p, norm_w, w_in, conv_w, conv_b, ml_b_i, ml_b_f, ml_norm_w, hg_lb, hg_norm_w,
           w_out, pe_norm_w, w_pg, w_pe, final_norm_w):
    batch, seq, _ = x.shape
    t = batch * seq
    chunk = min(CHUNK, seq)
    x2 = x.reshape(t, D_MODEL).astype(F32)
    p2 = p.reshape(-1, PE_DIM).astype(F32)

    w = w_in[0]
    n_gate = 2 * ML_HEADS
    gate_lo = 2 * ML_QK + 3 * ML_WIDTH
    w_main = jnp.concatenate([w[:, :gate_lo], w[:, gate_lo + n_gate:]], axis=1).astype(BF16)
    w_main = _col_tiles(w_main, TN_IN)
    w_gate = jnp.pad(w[:, gate_lo:gate_lo + n_gate].T, ((0, GATE_ROWS - n_gate), (0, 0))).astype(BF16)
    gbias = jnp.pad(jnp.concatenate([ml_b_i[0], ml_b_f[0]]).astype(F32), (0, GATE_ROWS - n_gate))
    gbias = jnp.broadcast_to(gbias[:, None], (GATE_ROWS, LANES))
    nw = norm_w[0].reshape(1, D_MODEL).astype(F32)

    y = _mixers(x2, nw, w_main, w_gate,
                conv_w[0].astype(F32), conv_b[0].reshape(1, -1).astype(F32), gbias,
                ml_norm_w[0].reshape(1, -1).astype(F32), hg_lb.astype(F32),
                hg_norm_w[0].reshape(1, -1).astype(F32), chunk, seq // chunk)
    out = _out_stage(x2, y, p2, _to_bf16(w_out[0]), _to_bf16(w_pg[0]), _to_bf16(w_pe[0]),
                     pe_norm_w[0].reshape(1, -1).astype(F32),
                     final_norm_w.reshape(1, -1).astype(F32))
    return out.reshape(batch, seq, D_MODEL).astype(x.dtype)
```

```python
import functools
import math

import jax
import jax.numpy as jnp
from jax import lax
from jax.experimental import pallas as pl
from jax.experimental.pallas import tpu as pltpu

F32 = jnp.float32
BF16 = jnp.bfloat16

D_MODEL = 2048
PE_DIM = 256
ML_HEADS = 4
ML_DQK = 128
ML_DV = 256
ML_QK = ML_HEADS * ML_DQK
ML_WIDTH = ML_HEADS * ML_DV
ML_CONV = 4
HG_HEADS = 8
HG_E = 128
HG_DV = 128
HG_WIDTH = HG_HEADS * HG_DV
EPS = 1e-6

MAIN_COLS = 8192
COL_QK = 0
COL_V = 1024
COL_O = 2048
COL_Z = 3072
COL_HQ = 4096
COL_HF = 5120
COL_HI = 6144
COL_HG = 7168
GATE_ROWS = 16

LANES = 128
SUBLANES = 8
NEG_BIG = -1e30
V7X_VMEM_LIMIT = 60 * 1024 * 1024

CHUNK = 128
TN_IN = 256
TM_OUT = 512
OUT_SPLIT = 2


def _dot(a, b):
    return jnp.dot(a, b, preferred_element_type=F32)


def _dot_nt(a, b):
    return lax.dot_general(a, b, (((1,), (1,)), ((), ())), preferred_element_type=F32)


def _dot_tn(a, b):
    return lax.dot_general(a, b, (((0,), (0,)), ((), ())), preferred_element_type=F32)


def _col_tiles(w, tn):
    k, n = w.shape
    return w.reshape(k, n // tn, tn).transpose(1, 0, 2)


def _rms_scale(x):
    return lax.rsqrt(jnp.mean(x * x, axis=-1, keepdims=True) + EPS)


def _scan_lanes(x, op, fill):
    n = x.shape[1]
    lane = lax.broadcasted_iota(jnp.int32, x.shape, 1)
    s = 1
    while s < n:
        shifted = pltpu.roll(x, s, axis=1)
        x = op(x, jnp.where(lane >= s, shifted, fill))
        s *= 2
    return x


def _sigmoid(x):
    return 1.0 / (1.0 + jnp.exp(-x))


def _silu(x):
    return x * _sigmoid(x)


def _normalise(x, nw_ref, u_ref):
    u_ref[...] = (x * _rms_scale(x) * nw_ref[...]).astype(BF16)


def _cols(slab_ref, c0, c1):
    tn = slab_ref.shape[2]
    parts = []
    while c0 < c1:
        t, off = divmod(c0, tn)
        n = min(c1 - c0, tn - off)
        parts.append(slab_ref[t, :, off:off + n])
        c0 += n
    return parts[0] if len(parts) == 1 else jnp.concatenate(parts, axis=1)


def _projection_tasks(u_ref, w_ref, wg_ref, proj_dst, gate_dst):
    n_tiles = w_ref.shape[0]

    def gates():
        gate_dst[...] = _dot_nt(wg_ref[...], u_ref[...])

    def tile(j):
        proj_dst[j] = _dot(u_ref[...], w_ref[j])

    return [gates] + [functools.partial(tile, j) for j in range(n_tiles)]


class _Filler:
    def __init__(self, tasks, slots):
        self._tasks = list(tasks)
        self._total = len(self._tasks)
        self._slots = slots
        self._used = 0

    def __call__(self, n=1):
        self._used += n
        due = -(-self._total * self._used // self._slots)
        while self._total - len(self._tasks) < due and self._tasks:
            self._tasks.pop(0)()

    def flush(self):
        while self._tasks:
            self._tasks.pop(0)()


def _mlstm_gates(gate_ref, gbias_ref, m_ref, L):
    i8 = gate_ref[0:SUBLANES, :] + gbias_ref[0:SUBLANES, 0:1]
    f8 = pltpu.roll(i8, ML_HEADS, axis=0)
    lf8 = jnp.minimum(f8, 0.0) - jnp.log1p(jnp.exp(-jnp.abs(f8)))
    b8 = _scan_lanes(lf8, jnp.add, 0.0)
    a8 = i8 - b8
    cm8 = _scan_lanes(a8, jnp.maximum, NEG_BIG)
    mprev8 = m_ref[:, 0:1]
    mm8 = jnp.maximum(mprev8, cm8)
    mlast8 = jnp.max(mm8, axis=1, keepdims=True)
    g8 = jnp.sum(lf8, axis=1, keepdims=True)
    m_ref[...] = jnp.broadcast_to(g8 + mlast8, m_ref.shape)
    table = jnp.concatenate(
        [mm8, b8, a8,
         jnp.broadcast_to(mlast8, (SUBLANES, L)),
         jnp.broadcast_to(mprev8, (SUBLANES, L)),
         jnp.zeros((LANES - 5 * SUBLANES, L), F32)], axis=0)
    return a8, table.T


def _mlstm_head(h, q_all, k_all, qk_h, a8, ct, causal, proj_ref, mlnw_ref, c_ref, n_ref, y_ref, rows,
                fill):
    mm_c = ct[:, h:h + 1]
    b_c = ct[:, 8 + h:9 + h]
    a_c = ct[:, 16 + h:17 + h]
    mlast_c = ct[:, 24 + h:25 + h]
    mprev_c = ct[:, 32 + h:33 + h]
    a_r = a8[h:h + 1, :]

    qh = q_all[:, h * ML_DQK:(h + 1) * ML_DQK]
    kh = k_all[:, h * ML_DQK:(h + 1) * ML_DQK]
    vb = _cols(proj_ref, COL_V + h * ML_DV, COL_V + (h + 1) * ML_DV).astype(BF16)

    decay = jnp.exp(jnp.where(causal, a_r - mm_c, NEG_BIG))
    scores = qk_h * decay
    w_inter = jnp.exp(mprev_c - mm_c)
    c_old = c_ref[h]
    n_old = n_ref[h]
    num = _dot(scores.astype(BF16), vb) + w_inter * _dot(qh.astype(BF16), c_old.astype(BF16))
    den = (jnp.sum(scores, axis=-1, keepdims=True)
           + w_inter * jnp.sum(qh * n_old[0:1, :], axis=-1, keepdims=True))
    hh = num / jnp.maximum(jnp.abs(den), jnp.exp(-(b_c + mm_c)))
    fill()

    wa_c = jnp.exp(a_c - mlast_c)
    ws_c = jnp.exp(mprev_c - mlast_c)
    kw = kh * wa_c
    c_ref[h] = ws_c[0:ML_DQK, :] * c_old + _dot_tn(kw.astype(BF16), vb)
    n_ref[h] = ws_c[0:SUBLANES, :] * n_old + jnp.sum(kw, axis=0, keepdims=True)

    hn = hh * _rms_scale(hh) * mlnw_ref[:, h * ML_DV:(h + 1) * ML_DV]
    o_gate = _sigmoid(_cols(proj_ref, COL_O + h * ML_DV, COL_O + (h + 1) * ML_DV))
    z = _cols(proj_ref, COL_Z + h * ML_DV, COL_Z + (h + 1) * ML_DV)
    y_ref[rows, h * ML_DV:(h + 1) * ML_DV] = (o_gate * hn * _silu(z)).astype(y_ref.dtype)


def _split_halves(a, m):
    n, w = a.shape
    a4 = a.reshape(n // (2 * m), 2, m, w)
    return a4[:, 0], a4[:, 1]


def _merge_halves(lo, hi):
    nb, m, w = lo.shape
    return jnp.stack([lo, hi], axis=1).reshape(2 * nb * m, w)


def _block_diag(a):
    z = jnp.zeros((a.shape[0], LANES), a.dtype)
    return jnp.concatenate([jnp.concatenate([a[:, :LANES], z], axis=1),
                            jnp.concatenate([z, a[:, LANES:]], axis=1)], axis=0)


def _block_diag_t(a):
    at = a.T.astype(BF16)
    z = jnp.zeros((LANES, LANES), BF16)
    return jnp.concatenate([jnp.concatenate([at[:LANES], z], axis=1),
                            jnp.concatenate([z, at[LANES:]], axis=1)], axis=0)


def _hgrn2_pair(hp, lb_all, proj_ref, hgnw_ref, st_ref, lv_ref, y_ref, rows, fill, fill_levels):
    L = proj_ref.shape[1]
    nlev = int(math.log2(L))
    width = 2 * HG_E
    sl = slice(hp * width, (hp + 1) * width)
    lb = lb_all[:, sl]
    q = _cols(proj_ref, COL_HQ + hp * width, COL_HQ + (hp + 1) * width)
    v = _cols(proj_ref, COL_HI + hp * width, COL_HI + (hp + 1) * width)
    f_pre = _cols(proj_ref, COL_HF + hp * width, COL_HF + (hp + 1) * width)
    kk = (1.0 - lb) * _sigmoid(-f_pre)
    c = jnp.log(lb + (1.0 - lb) * _sigmoid(f_pre))

    q8 = q.reshape(L // SUBLANES, SUBLANES, width)
    kk8 = kk.reshape(L // SUBLANES, SUBLANES, width)
    sub = lax.broadcasted_iota(jnp.int32, q8.shape, 1)
    operands = []
    for lev in range(nlev):
        m = 1 << lev
        if m < SUBLANES:
            c8 = c.reshape(L // SUBLANES, SUBLANES, width)
            upper = (sub & m) != 0
            tot = c8
            s = 1
            while s < m:
                tot = jnp.where((sub & s) != 0, tot, pltpu.roll(tot, SUBLANES - s, axis=1))
                s *= 2
            e = jnp.exp(jnp.where(upper, c8, tot - c8))
            zz = (jnp.where(upper, q8, kk8) * e).reshape(L, width)
            operands.append((zz.astype(BF16), _block_diag_t(zz)))
            c = (c8 + jnp.where(upper, pltpu.roll(tot, m, axis=1), 0.0)).reshape(L, width)
        else:
            c_lo, c_hi = _split_halves(c, m)
            tot_lo = jnp.broadcast_to(c_lo[:, m - 1:m, :], c_lo.shape)
            z_lo = _split_halves(kk, m)[0] * jnp.exp(tot_lo - c_lo)
            z_hi = _split_halves(q, m)[1] * jnp.exp(c_hi)
            operands.append((z_hi.reshape(L // 2, width).astype(BF16),
                             _block_diag_t(_merge_halves(z_lo, z_hi))))
            c = _merge_halves(c_lo, c_hi + tot_lo)
        if lev in fill_levels:
            fill()

    att = None
    for lev, (lhs, rhs) in enumerate(operands):
        m = 1 << lev
        p = _dot(lhs, rhs)
        if m < SUBLANES:
            p = jnp.where(lv_ref[...] == lev, p, 0.0)
            att = p if att is None else att + p
        else:
            lv_hi = _split_halves(lv_ref[...], m)[1].reshape(L // 2, 2 * L)
            p = jnp.where(lv_hi == lev, p, 0.0).reshape(L // (2 * m), m, 2 * L)
            att_lo, att_hi = _split_halves(att, m)
            att = _merge_halves(att_lo, att_hi + p)

    tot = jnp.broadcast_to(c[L - 1:L, :], c.shape)
    st_old = st_ref[hp]
    vb = v.astype(BF16)
    qk_diag = q * kk
    o = (_dot(att.astype(BF16), _block_diag(vb))
         + _dot((q * jnp.exp(c)).astype(BF16), _block_diag_t(st_old)))
    ke = (kk * jnp.exp(tot - c)).astype(BF16)
    both = _dot_tn(vb, ke)
    upd = jnp.concatenate([both[:HG_DV, :HG_E], both[HG_DV:, HG_E:]], axis=1)
    st_ref[hp] = st_old * jnp.exp(tot[0:HG_DV, :]) + upd

    gz = _cols(proj_ref, COL_HG + hp * width, COL_HG + (hp + 1) * width)
    gate = _silu(gz)
    nw = hgnw_ref[:, sl]
    for i in range(2):
        hs = slice(i * HG_DV, (i + 1) * HG_DV)
        oi = o[:, hs] + jnp.sum(qk_diag[:, hs], axis=-1, keepdims=True) * v[:, hs]
        on = oi * _rms_scale(oi) * nw[:, hs]
        col = ML_WIDTH + hp * width + i * HG_DV
        y_ref[rows, col:col + HG_DV] = (on * gate[:, hs]).astype(y_ref.dtype)


def _mix(proj_ref, gate_ref, convw_ref, convb_ref, gbias_ref, mlnw_ref, hglb_ref, hgnw_ref,
         y_ref, r0, qkext_ref, c_ref, n_ref, m_ref, st_ref, lv_ref, tasks, late_task):
    L = proj_ref.shape[1]
    rows = slice(r0, r0 + L)
    fill = _Filler(tasks, slots=6 + 5 * (HG_HEADS // 2) + 2 * ML_HEADS)

    fill(2)
    a8, ct = _mlstm_gates(gate_ref, gbias_ref, m_ref, L)
    fill(2)

    qkext_ref[SUBLANES:SUBLANES + L, :] = _cols(proj_ref, COL_QK, COL_QK + 2 * ML_QK)
    conv = convb_ref[...]
    for k in range(ML_CONV):
        off = SUBLANES - (ML_CONV - 1) + k
        conv = conv + qkext_ref[off:off + L, :] * convw_ref[k:k + 1, :]
    qkext_ref[0:SUBLANES, :] = qkext_ref[L:L + SUBLANES, :]
    qk = _silu(conv)
    q_all = qk[:, :ML_QK]
    k_all = qk[:, ML_QK:] * (ML_DQK ** -0.5)
    fill(2)

    lbl = hglb_ref[...]
    lmax = jnp.max(lbl, axis=0, keepdims=True)
    lexp = jnp.exp(lbl - lmax)
    lb_all = lexp[0:1, :] / jnp.sum(lexp, axis=0, keepdims=True)
    for hp in range(HG_HEADS // 2):
        _hgrn2_pair(hp, lb_all, proj_ref, hgnw_ref, st_ref, lv_ref, y_ref, rows, fill,
                    fill_levels=(0, 1, 3, 4, 6))
        if hp == HG_HEADS // 4 - 1:
            late_task()

    rowi = lax.broadcasted_iota(jnp.int32, (L, L), 0)
    coli = lax.broadcasted_iota(jnp.int32, (L, L), 1)
    causal = coli <= rowi
    for h in range(ML_HEADS):
        if h % 2 == 0:
            pair = slice(h * ML_DQK, (h + 2) * ML_DQK)
            qk_pair = _dot(q_all[:, pair].astype(BF16), _block_diag_t(k_all[:, pair]))
        qk_h = qk_pair[:, (h % 2) * L:(h % 2 + 1) * L]
        _mlstm_head(h, q_all, k_all, qk_h, a8, ct, causal, proj_ref, mlnw_ref, c_ref, n_ref, y_ref,
                    rows, fill)
        fill()
    fill.flush()


def _mixer_kernel(steps_per_seq,
                  x0_ref, xa_ref, xb_ref, nw_ref, w_ref, wg_ref,
                  convw_ref, convb_ref, gbias_ref, mlnw_ref, hglb_ref, hgnw_ref,
                  y_ref,
                  proj_a, proj_b, gate_a, gate_b, u_a, u_b,
                  qkext_ref, c_ref, n_ref, m_ref, st_ref, lv_ref):
    L = proj_a.shape[1]
    g = pl.program_id(0)

    @pl.when(g == 0)
    def _():
        _normalise(x0_ref[0:L, :], nw_ref, u_a)
        gate_a[...] = _dot_nt(wg_ref[...], u_a[...])

        def project_tile(j, carry):
            proj_a[j] = _dot(u_a[...], w_ref[j])
            return carry

        lax.fori_loop(0, w_ref.shape[0], project_tile, 0)
        _normalise(x0_ref[L:2 * L, :], nw_ref, u_b)
        r = lax.broadcasted_iota(jnp.int32, (L, L), 0)
        c = lax.broadcasted_iota(jnp.int32, (L, L), 1)
        lv = 31 - lax.clz(r ^ c)
        lv = jnp.where(r > c, lv, -1)
        lv_ref[...] = jnp.concatenate([lv, lv], axis=1)

    @pl.when(g % steps_per_seq == 0)
    def _():
        qkext_ref[0:SUBLANES, :] = jnp.zeros((SUBLANES, 2 * ML_QK), F32)
        c_ref[...] = jnp.zeros(c_ref.shape, F32)
        n_ref[...] = jnp.zeros(n_ref.shape, F32)
        m_ref[...] = jnp.zeros(m_ref.shape, F32)
        st_ref[...] = jnp.zeros(st_ref.shape, F32)

    params = (convw_ref, convb_ref, gbias_ref, mlnw_ref, hglb_ref, hgnw_ref)
    state = (qkext_ref, c_ref, n_ref, m_ref, st_ref, lv_ref)
    _mix(proj_a, gate_a, *params, y_ref, 0, *state,
         _projection_tasks(u_b, w_ref, wg_ref, proj_b, gate_b),
         lambda: _normalise(xa_ref[...], nw_ref, u_a))
    _mix(proj_b, gate_b, *params, y_ref, L, *state,
         _projection_tasks(u_a, w_ref, wg_ref, proj_a, gate_a),
         lambda: _normalise(xb_ref[...], nw_ref, u_b))


def _mixers(x2, norm_w, w_main, w_gate, conv_w, conv_b, gbias, ml_norm_w, hg_lb,
            hg_norm_w, chunk, chunks_per_seq):
    t = x2.shape[0]
    L = chunk
    nchunks = t // L
    steps = nchunks // 2
    assert chunks_per_seq % 2 == 0
    const = lambda shape: pl.BlockSpec(shape, lambda g: (0, 0), pipeline_mode=pl.Buffered(1))
    return pl.pallas_call(
        functools.partial(_mixer_kernel, chunks_per_seq // 2),
        grid=(steps,),
        in_specs=[
            const((2 * L, D_MODEL)),
            pl.BlockSpec((L, D_MODEL), lambda g: (jnp.minimum(2 * g + 2, nchunks - 1), 0)),
            pl.BlockSpec((L, D_MODEL), lambda g: (jnp.minimum(2 * g + 3, nchunks - 1), 0)),
            const((1, D_MODEL)),
            pl.BlockSpec(w_main.shape, lambda g: (0, 0, 0), pipeline_mode=pl.Buffered(1)),
            const((GATE_ROWS, D_MODEL)),
            const((ML_CONV, 2 * ML_QK)),
            const((1, 2 * ML_QK)),
            const((GATE_ROWS, LANES)),
            const((1, ML_WIDTH)),
            const((2, HG_WIDTH)),
            const((1, HG_WIDTH)),
        ],
        out_specs=pl.BlockSpec((2 * L, ML_WIDTH + HG_WIDTH), lambda g: (g, 0)),
        out_shape=jax.ShapeDtypeStruct((t, ML_WIDTH + HG_WIDTH), BF16),
        scratch_shapes=[
            pltpu.VMEM((MAIN_COLS // TN_IN, L, TN_IN), F32),
            pltpu.VMEM((MAIN_COLS // TN_IN, L, TN_IN), F32),
            pltpu.VMEM((GATE_ROWS, L), F32),
            pltpu.VMEM((GATE_ROWS, L), F32),
            pltpu.VMEM((L, D_MODEL), BF16),
            pltpu.VMEM((L, D_MODEL), BF16),
            pltpu.VMEM((L + SUBLANES, 2 * ML_QK), F32),
            pltpu.VMEM((ML_HEADS, ML_DQK, ML_DV), F32),
            pltpu.VMEM((ML_HEADS, SUBLANES, ML_DQK), F32),
            pltpu.VMEM((SUBLANES, LANES), F32),
            pltpu.VMEM((HG_HEADS // 2, HG_DV, 2 * HG_E), F32),
            pltpu.VMEM((L, 2 * L), jnp.int32),
        ],
        compiler_params=pltpu.CompilerParams(
            dimension_semantics=("arbitrary",),
            vmem_limit_bytes=V7X_VMEM_LIMIT),
        name="mixers",
    )(x2, x2, x2, norm_w, w_main, w_gate, conv_w, conv_b, gbias, ml_norm_w, hg_lb, hg_norm_w)


def _out_kernel(x_ref, y_ref, p_ref, wout_ref, wpg_ref, wpe_ref, pnw_ref, fnw_ref, o_ref):
    half = x_ref.shape[0] // OUT_SPLIT
    parts = [slice(i * half, (i + 1) * half) for i in range(OUT_SPLIT)]
    hs = [x_ref[rows, :] + _dot(y_ref[rows, :], wout_ref[...]) for rows in parts]
    for rows, h in zip(parts, hs):
        hn = h * _rms_scale(h) * pnw_ref[...]
        gate = _sigmoid(_dot(hn.astype(BF16), wpg_ref[...]))
        emb = _dot(p_ref[rows, :].astype(BF16), wpe_ref[...])
        h = h + gate * emb
        o_ref[rows, :] = h * _rms_scale(h) * fnw_ref[...]


def _out_stage(x2, y, p2, w_out, w_pg, w_pe, pe_norm_w, final_norm_w):
    t = x2.shape[0]
    tm = min(TM_OUT, t)
    const = lambda shape: pl.BlockSpec(shape, lambda i: (0, 0), pipeline_mode=pl.Buffered(1))
    return pl.pallas_call(
        _out_kernel,
        grid=(t // tm,),
        in_specs=[
            pl.BlockSpec((tm, D_MODEL), lambda i: (i, 0)),
            pl.BlockSpec((tm, D_MODEL), lambda i: (i, 0)),
            pl.BlockSpec((tm, PE_DIM), lambda i: (i, 0)),
            const(w_out.shape),
            const(w_pg.shape),
            const(w_pe.shape),
            const((1, D_MODEL)),
            const((1, D_MODEL)),
        ],
        out_specs=pl.BlockSpec((tm, D_MODEL), lambda i: (i, 0)),
        out_shape=jax.ShapeDtypeStruct((t, D_MODEL), F32),
        compiler_params=pltpu.CompilerParams(
            dimension_semantics=("arbitrary",),
            vmem_limit_bytes=V7X_VMEM_LIMIT),
        name="out_stage",
    )(x2, y, p2, w_out, w_pg, w_pe, pe_norm_w, final_norm_w)


def kernel(x, p, norm_w, w_in, conv_w, conv_b, ml_b_i, ml_b_f, ml_norm_w, hg_lb, hg_norm_w,
           w_out, pe_norm_w, w_pg, w_pe, final_norm_w):
    batch, seq, _ = x.shape
    t = batch * seq
    chunk = min(CHUNK, seq)
    x2 = x.reshape(t, D_MODEL).astype(F32)
    p2 = p.reshape(-1, PE_DIM).astype(F32)

    w = w_in[0]
    n_gate = 2 * ML_HEADS
    gate_lo = 2 * ML_QK + 3 * ML_WIDTH
    w_main = jnp.concatenate([w[:, :gate_lo], w[:, gate_lo + n_gate:]], axis=1).astype(BF16)
    w_main = _col_tiles(w_main, TN_IN)
    w_gate = jnp.pad(w[:, gate_lo:gate_lo + n_gate].T, ((0, GATE_ROWS - n_gate), (0, 0))).astype(BF16)
    gbias = jnp.pad(jnp.concatenate([ml_b_i[0], ml_b_f[0]]).astype(F32), (0, GATE_ROWS - n_gate))
    gbias = jnp.broadcast_to(gbias[:, None], (GATE_ROWS, LANES))
    nw = norm_w[0].reshape(1, D_MODEL).astype(F32)

    y = _mixers(x2, nw, w_main, w_gate,
                conv_w[0].astype(F32), conv_b[0].reshape(1, -1).astype(F32), gbias,
                ml_norm_w[0].reshape(1, -1).astype(F32), hg_lb.astype(F32),
                hg_norm_w[0].reshape(1, -1).astype(F32), chunk, seq // chunk)
    out = _out_stage(x2, y, p2, w_out[0].astype(BF16), w_pg[0].astype(BF16), w_pe[0].astype(BF16),
                     pe_norm_w[0].reshape(1, -1).astype(F32),
                     final_norm_w.reshape(1, -1).astype(F32))
    return out.reshape(batch, seq, D_MODEL).astype(x.dtype)
```

```python
import functools
import math

import jax
import jax.numpy as jnp
from jax import lax
from jax.experimental import pallas as pl
from jax.experimental.pallas import tpu as pltpu

F32 = jnp.float32
BF16 = jnp.bfloat16

D_MODEL = 2048
PE_DIM = 256
ML_HEADS = 4
ML_DQK = 128
ML_DV = 256
ML_QK = ML_HEADS * ML_DQK
ML_WIDTH = ML_HEADS * ML_DV
ML_CONV = 4
HG_HEADS = 8
HG_E = 128
HG_DV = 128
HG_WIDTH = HG_HEADS * HG_DV
EPS = 1e-6

MAIN_COLS = 8192
COL_QK = 0
COL_V = 1024
COL_O = 2048
COL_Z = 3072
COL_HQ = 4096
COL_HF = 5120
COL_HI = 6144
COL_HG = 7168
GATE_ROWS = 16

LANES = 128
SUBLANES = 8
NEG_BIG = -1e30
V7X_VMEM_LIMIT = 58 * 1024 * 1024

CHUNK = 128
TN_IN = 512
TM_OUT = 512
OUT_SPLIT = 2


def _dot(a, b):
    return jnp.dot(a, b, preferred_element_type=F32)


def _dot_nt(a, b):
    return lax.dot_general(a, b, (((1,), (1,)), ((), ())), preferred_element_type=F32)


def _dot_tn(a, b):
    return lax.dot_general(a, b, (((0,), (0,)), ((), ())), preferred_element_type=F32)


def _col_tiles(w, tn):
    k, n = w.shape
    return w.reshape(k, n // tn, tn).transpose(1, 0, 2)


def _rms_scale(x):
    return lax.rsqrt(jnp.mean(x * x, axis=-1, keepdims=True) + EPS)


def _scan_lanes(x, op, fill):
    n = x.shape[1]
    lane = lax.broadcasted_iota(jnp.int32, x.shape, 1)
    s = 1
    while s < n:
        shifted = pltpu.roll(x, s, axis=1)
        x = op(x, jnp.where(lane >= s, shifted, fill))
        s *= 2
    return x


def _sigmoid(x):
    return 1.0 / (1.0 + jnp.exp(-x))


def _silu(x):
    return x * _sigmoid(x)


def _normalise(x, nw_ref, u_ref):
    u_ref[...] = (x * _rms_scale(x) * nw_ref[...]).astype(BF16)


def _cols(slab_ref, c0, c1):
    tn = slab_ref.shape[2]
    parts = []
    while c0 < c1:
        t, off = divmod(c0, tn)
        n = min(c1 - c0, tn - off)
        parts.append(slab_ref[t, :, off:off + n])
        c0 += n
    return parts[0] if len(parts) == 1 else jnp.concatenate(parts, axis=1)


def _projection_tasks(u_ref, w_ref, wg_ref, proj_dst, gate_dst):
    n_tiles = w_ref.shape[0]

    def gates():
        gate_dst[...] = _dot_nt(wg_ref[...], u_ref[...])

    def tile(j):
        proj_dst[j] = _dot(u_ref[...], w_ref[j])

    return [gates] + [functools.partial(tile, j) for j in range(n_tiles)]


class _Filler:
    def __init__(self, tasks, slots):
        self._tasks = list(tasks)
        self._total = len(self._tasks)
        self._slots = slots
        self._used = 0

    def __call__(self, n=1):
        self._used += n
        due = -(-self._total * self._used // self._slots)
        while self._total - len(self._tasks) < due and self._tasks:
            self._tasks.pop(0)()

    def flush(self):
        while self._tasks:
            self._tasks.pop(0)()


def _mlstm_gates(gate_ref, gbias_ref, m_ref, L):
    i8 = gate_ref[0:SUBLANES, :] + gbias_ref[0:SUBLANES, 0:1]
    f8 = pltpu.roll(i8, ML_HEADS, axis=0)
    lf8 = jnp.minimum(f8, 0.0) - jnp.log1p(jnp.exp(-jnp.abs(f8)))
    b8 = _scan_lanes(lf8, jnp.add, 0.0)
    a8 = i8 - b8
    cm8 = _scan_lanes(a8, jnp.maximum, NEG_BIG)
    mprev8 = m_ref[:, 0:1]
    mm8 = jnp.maximum(mprev8, cm8)
    mlast8 = jnp.max(mm8, axis=1, keepdims=True)
    g8 = jnp.sum(lf8, axis=1, keepdims=True)
    m_ref[...] = jnp.broadcast_to(g8 + mlast8, m_ref.shape)
    table = jnp.concatenate(
        [mm8, b8, a8,
         jnp.broadcast_to(mlast8, (SUBLANES, L)),
         jnp.broadcast_to(mprev8, (SUBLANES, L)),
         jnp.zeros((LANES - 5 * SUBLANES, L), F32)], axis=0)
    return a8, table.T


def _mlstm_head(h, q_all, k_all, qk_h, a8, ct, causal, proj_ref, mlnw_ref, c_ref, n_ref, y_ref, rows,
                fill):
    mm_c = ct[:, h:h + 1]
    b_c = ct[:, 8 + h:9 + h]
    a_c = ct[:, 16 + h:17 + h]
    mlast_c = ct[:, 24 + h:25 + h]
    mprev_c = ct[:, 32 + h:33 + h]
    a_r = a8[h:h + 1, :]

    qh = q_all[:, h * ML_DQK:(h + 1) * ML_DQK]
    kh = k_all[:, h * ML_DQK:(h + 1) * ML_DQK]
    vb = _cols(proj_ref, COL_V + h * ML_DV, COL_V + (h + 1) * ML_DV).astype(BF16)

    decay = jnp.exp(jnp.where(causal, a_r - mm_c, NEG_BIG))
    scores = qk_h * decay
    w_inter = jnp.exp(mprev_c - mm_c)
    c_old = c_ref[h]
    n_old = n_ref[h]
    num = _dot(scores.astype(BF16), vb) + w_inter * _dot(qh.astype(BF16), c_old.astype(BF16))
    den = (jnp.sum(scores, axis=-1, keepdims=True)
           + w_inter * jnp.sum(qh * n_old[0:1, :], axis=-1, keepdims=True))
    hh = num / jnp.maximum(jnp.abs(den), jnp.exp(-(b_c + mm_c)))
    fill()

    wa_c = jnp.exp(a_c - mlast_c)
    ws_c = jnp.exp(mprev_c - mlast_c)
    kw = kh * wa_c
    c_ref[h] = ws_c[0:ML_DQK, :] * c_old + _dot_tn(kw.astype(BF16), vb)
    n_ref[h] = ws_c[0:SUBLANES, :] * n_old + jnp.sum(kw, axis=0, keepdims=True)

    hn = hh * _rms_scale(hh) * mlnw_ref[:, h * ML_DV:(h + 1) * ML_DV]
    o_gate = _sigmoid(_cols(proj_ref, COL_O + h * ML_DV, COL_O + (h + 1) * ML_DV))
    z = _cols(proj_ref, COL_Z + h * ML_DV, COL_Z + (h + 1) * ML_DV)
    y_ref[rows, h * ML_DV:(h + 1) * ML_DV] = (o_gate * hn * _silu(z)).astype(y_ref.dtype)


def _split_halves(a, m):
    n, w = a.shape
    a4 = a.reshape(n // (2 * m), 2, m, w)
    return a4[:, 0], a4[:, 1]


def _merge_halves(lo, hi):
    nb, m, w = lo.shape
    return jnp.stack([lo, hi], axis=1).reshape(2 * nb * m, w)


def _block_diag(a):
    z = jnp.zeros((a.shape[0], LANES), a.dtype)
    return jnp.concatenate([jnp.concatenate([a[:, :LANES], z], axis=1),
                            jnp.concatenate([z, a[:, LANES:]], axis=1)], axis=0)


def _hgrn2_pair(hp, lb_all, proj_ref, hgnw_ref, st_ref, lv_ref, y_ref, rows, fill, fill_levels):
    L = proj_ref.shape[1]
    nlev = int(math.log2(L))
    width = 2 * HG_E
    sl = slice(hp * width, (hp + 1) * width)
    lb = lb_all[:, sl]
    q = _cols(proj_ref, COL_HQ + hp * width, COL_HQ + (hp + 1) * width)
    v = _cols(proj_ref, COL_HI + hp * width, COL_HI + (hp + 1) * width)
    f_pre = _cols(proj_ref, COL_HF + hp * width, COL_HF + (hp + 1) * width)
    kk = (1.0 - lb) * _sigmoid(-f_pre)
    c = jnp.log(lb + (1.0 - lb) * _sigmoid(f_pre))

    att = None
    q8 = q.reshape(L // SUBLANES, SUBLANES, width)
    kk8 = kk.reshape(L // SUBLANES, SUBLANES, width)
    sub = lax.broadcasted_iota(jnp.int32, q8.shape, 1)
    for lev in range(nlev):
        m = 1 << lev
        if m < SUBLANES:
            c8 = c.reshape(L // SUBLANES, SUBLANES, width)
            upper = (sub & m) != 0
            tot = c8
            s = 1
            while s < m:
                tot = jnp.where((sub & s) != 0, tot, pltpu.roll(tot, SUBLANES - s, axis=1))
                s *= 2
            e = jnp.exp(jnp.where(upper, c8, tot - c8))
            zz = (jnp.where(upper, q8, kk8) * e).reshape(L, width).astype(BF16)
            p = jnp.where(lv_ref[...] == lev, _dot_nt(zz, _block_diag(zz)), 0.0)
            att = p if att is None else att + p
            c = (c8 + jnp.where(upper, pltpu.roll(tot, m, axis=1), 0.0)).reshape(L, width)
        else:
            c_lo, c_hi = _split_halves(c, m)
            tot_lo = jnp.broadcast_to(c_lo[:, m - 1:m, :], c_lo.shape)
            z_lo = _split_halves(kk, m)[0] * jnp.exp(tot_lo - c_lo)
            z_hi = _split_halves(q, m)[1] * jnp.exp(c_hi)
            zz = _merge_halves(z_lo, z_hi).astype(BF16)
            p = _dot_nt(z_hi.reshape(L // 2, width).astype(BF16), _block_diag(zz))
            lv_hi = _split_halves(lv_ref[...], m)[1].reshape(L // 2, 2 * L)
            p = jnp.where(lv_hi == lev, p, 0.0).reshape(L // (2 * m), m, 2 * L)
            att_lo, att_hi = _split_halves(att, m)
            att = _merge_halves(att_lo, att_hi + p)
            c = _merge_halves(c_lo, c_hi + tot_lo)
        if lev in fill_levels:
            fill()

    tot = jnp.broadcast_to(c[L - 1:L, :], c.shape)
    st_old = st_ref[hp]
    vb = v.astype(BF16)
    qk_diag = q * kk
    o = (_dot(att.astype(BF16), _block_diag(vb))
         + _dot_nt((q * jnp.exp(c)).astype(BF16), _block_diag(st_old.astype(BF16))))
    ke = (kk * jnp.exp(tot - c)).astype(BF16)
    upd = jnp.concatenate([_dot_tn(vb[:, :HG_DV], ke[:, :HG_E]),
                           _dot_tn(vb[:, HG_DV:], ke[:, HG_E:])], axis=1)
    st_ref[hp] = st_old * jnp.exp(tot[0:HG_DV, :]) + upd

    gz = _cols(proj_ref, COL_HG + hp * width, COL_HG + (hp + 1) * width)
    gate = _silu(gz)
    nw = hgnw_ref[:, sl]
    for i in range(2):
        hs = slice(i * HG_DV, (i + 1) * HG_DV)
        oi = o[:, hs] + jnp.sum(qk_diag[:, hs], axis=-1, keepdims=True) * v[:, hs]
        on = oi * _rms_scale(oi) * nw[:, hs]
        col = ML_WIDTH + hp * width + i * HG_DV
        y_ref[rows, col:col + HG_DV] = (on * gate[:, hs]).astype(y_ref.dtype)


def _mix(proj_ref, gate_ref, convw_ref, convb_ref, gbias_ref, mlnw_ref, hglb_ref, hgnw_ref,
         y_ref, r0, qkext_ref, c_ref, n_ref, m_ref, st_ref, lv_ref, tasks, late_task):
    L = proj_ref.shape[1]
    rows = slice(r0, r0 + L)
    fill = _Filler(tasks, slots=6 + 5 * (HG_HEADS // 2) + 2 * ML_HEADS)

    fill(2)
    a8, ct = _mlstm_gates(gate_ref, gbias_ref, m_ref, L)
    fill(2)

    qkext_ref[SUBLANES:SUBLANES + L, :] = _cols(proj_ref, COL_QK, COL_QK + 2 * ML_QK)
    conv = convb_ref[...]
    for k in range(ML_CONV):
        off = SUBLANES - (ML_CONV - 1) + k
        conv = conv + qkext_ref[off:off + L, :] * convw_ref[k:k + 1, :]
    qkext_ref[0:SUBLANES, :] = qkext_ref[L:L + SUBLANES, :]
    qk = _silu(conv)
    q_all = qk[:, :ML_QK]
    k_all = qk[:, ML_QK:] * (ML_DQK ** -0.5)
    fill(2)

    lbl = hglb_ref[...]
    lmax = jnp.max(lbl, axis=0, keepdims=True)
    lexp = jnp.exp(lbl - lmax)
    lb_all = lexp[0:1, :] / jnp.sum(lexp, axis=0, keepdims=True)
    for hp in range(HG_HEADS // 2):
        _hgrn2_pair(hp, lb_all, proj_ref, hgnw_ref, st_ref, lv_ref, y_ref, rows, fill,
                    fill_levels=(0, 1, 3, 4, 6))
        if hp == HG_HEADS // 4 - 1:
            late_task()

    rowi = lax.broadcasted_iota(jnp.int32, (L, L), 0)
    coli = lax.broadcasted_iota(jnp.int32, (L, L), 1)
    causal = coli <= rowi
    for h in range(ML_HEADS):
        if h % 2 == 0:
            pair = slice(h * ML_DQK, (h + 2) * ML_DQK)
            qk_pair = _dot_nt(q_all[:, pair].astype(BF16), _block_diag(k_all[:, pair].astype(BF16)))
        qk_h = qk_pair[:, (h % 2) * L:(h % 2 + 1) * L]
        _mlstm_head(h, q_all, k_all, qk_h, a8, ct, causal, proj_ref, mlnw_ref, c_ref, n_ref, y_ref,
                    rows, fill)
        fill()
    fill.flush()


def _mixer_kernel(steps_per_seq,
                  x0_ref, xa_ref, xb_ref, nw_ref, w_ref, wg_ref,
                  convw_ref, convb_ref, gbias_ref, mlnw_ref, hglb_ref, hgnw_ref,
                  y_ref,
                  proj_a, proj_b, gate_a, gate_b, u_a, u_b,
                  qkext_ref, c_ref, n_ref, m_ref, st_ref, lv_ref):
    L = proj_a.shape[1]
    g = pl.program_id(0)

    @pl.when(g == 0)
    def _():
        _normalise(x0_ref[0:L, :], nw_ref, u_a)
        gate_a[...] = _dot_nt(wg_ref[...], u_a[...])

        def project_tile(j, carry):
            proj_a[j] = _dot(u_a[...], w_ref[j])
            return carry

        lax.fori_loop(0, w_ref.shape[0], project_tile, 0)
        _normalise(x0_ref[L:2 * L, :], nw_ref, u_b)
        r = lax.broadcasted_iota(jnp.int32, (L, L), 0)
        c = lax.broadcasted_iota(jnp.int32, (L, L), 1)
        lv = 31 - lax.clz(r ^ c)
        lv = jnp.where(r > c, lv, -1)
        lv_ref[...] = jnp.concatenate([lv, lv], axis=1)

    @pl.when(g % steps_per_seq == 0)
    def _():
        qkext_ref[0:SUBLANES, :] = jnp.zeros((SUBLANES, 2 * ML_QK), F32)
        c_ref[...] = jnp.zeros(c_ref.shape, F32)
        n_ref[...] = jnp.zeros(n_ref.shape, F32)
        m_ref[...] = jnp.zeros(m_ref.shape, F32)
        st_ref[...] = jnp.zeros(st_ref.shape, F32)

    params = (convw_ref, convb_ref, gbias_ref, mlnw_ref, hglb_ref, hgnw_ref)
    state = (qkext_ref, c_ref, n_ref, m_ref, st_ref, lv_ref)
    _mix(proj_a, gate_a, *params, y_ref, 0, *state,
         _projection_tasks(u_b, w_ref, wg_ref, proj_b, gate_b),
         lambda: _normalise(xa_ref[...], nw_ref, u_a))
    _mix(proj_b, gate_b, *params, y_ref, L, *state,
         _projection_tasks(u_a, w_ref, wg_ref, proj_a, gate_a),
         lambda: _normalise(xb_ref[...], nw_ref, u_b))


def _mixers(x2, norm_w, w_main, w_gate, conv_w, conv_b, gbias, ml_norm_w, hg_lb,
            hg_norm_w, chunk, chunks_per_seq):
    t = x2.shape[0]
    L = chunk
    nchunks = t // L
    steps = nchunks // 2
    assert chunks_per_seq % 2 == 0
    const = lambda shape: pl.BlockSpec(shape, lambda g: (0, 0), pipeline_mode=pl.Buffered(1))
    return pl.pallas_call(
        functools.partial(_mixer_kernel, chunks_per_seq // 2),
        grid=(steps,),
        in_specs=[
            const((2 * L, D_MODEL)),
            pl.BlockSpec((L, D_MODEL), lambda g: (jnp.minimum(2 * g + 2, nchunks - 1), 0)),
            pl.BlockSpec((L, D_MODEL), lambda g: (jnp.minimum(2 * g + 3, nchunks - 1), 0)),
            const((1, D_MODEL)),
            pl.BlockSpec(w_main.shape, lambda g: (0, 0, 0), pipeline_mode=pl.Buffered(1)),
            const((GATE_ROWS, D_MODEL)),
            const((ML_CONV, 2 * ML_QK)),
            const((1, 2 * ML_QK)),
            const((GATE_ROWS, LANES)),
            const((1, ML_WIDTH)),
            const((2, HG_WIDTH)),
            const((1, HG_WIDTH)),
        ],
        out_specs=pl.BlockSpec((2 * L, ML_WIDTH + HG_WIDTH), lambda g: (g, 0)),
        out_shape=jax.ShapeDtypeStruct((t, ML_WIDTH + HG_WIDTH), BF16),
        scratch_shapes=[
            pltpu.VMEM((MAIN_COLS // TN_IN, L, TN_IN), F32),
            pltpu.VMEM((MAIN_COLS // TN_IN, L, TN_IN), F32),
            pltpu.VMEM((GATE_ROWS, L), F32),
            pltpu.VMEM((GATE_ROWS, L), F32),
            pltpu.VMEM((L, D_MODEL), BF16),
            pltpu.VMEM((L, D_MODEL), BF16),
            pltpu.VMEM((L + SUBLANES, 2 * ML_QK), F32),
            pltpu.VMEM((ML_HEADS, ML_DQK, ML_DV), F32),
            pltpu.VMEM((ML_HEADS, SUBLANES, ML_DQK), F32),
            pltpu.VMEM((SUBLANES, LANES), F32),
            pltpu.VMEM((HG_HEADS // 2, HG_DV, 2 * HG_E), F32),
            pltpu.VMEM((L, 2 * L), jnp.int32),
        ],
        compiler_params=pltpu.CompilerParams(
            dimension_semantics=("arbitrary",),
            vmem_limit_bytes=V7X_VMEM_LIMIT),
        name="mixers",
    )(x2, x2, x2, norm_w, w_main, w_gate, conv_w, conv_b, gbias, ml_norm_w, hg_lb, hg_norm_w)


def _out_kernel(x_ref, y_ref, p_ref, wout_ref, wpg_ref, wpe_ref, pnw_ref, fnw_ref, o_ref):
    half = x_ref.shape[0] // OUT_SPLIT
    parts = [slice(i * half, (i + 1) * half) for i in range(OUT_SPLIT)]
    hs = [x_ref[rows, :] + _dot(y_ref[rows, :], wout_ref[...]) for rows in parts]
    for rows, h in zip(parts, hs):
        hn = h * _rms_scale(h) * pnw_ref[...]
        gate = _sigmoid(_dot(hn.astype(BF16), wpg_ref[...]))
        emb = _dot(p_ref[rows, :].astype(BF16), wpe_ref[...])
        h = h + gate * emb
        o_ref[rows, :] = h * _rms_scale(h) * fnw_ref[...]


def _out_stage(x2, y, p2, w_out, w_pg, w_pe, pe_norm_w, final_norm_w):
    t = x2.shape[0]
    tm = min(TM_OUT, t)
    const = lambda shape: pl.BlockSpec(shape, lambda i: (0, 0), pipeline_mode=pl.Buffered(1))
    return pl.pallas_call(
        _out_kernel,
        grid=(t // tm,),
        in_specs=[
            pl.BlockSpec((tm, D_MODEL), lambda i: (i, 0)),
            pl.BlockSpec((tm, D_MODEL), lambda i: (i, 0)),
            pl.BlockSpec((tm, PE_DIM), lambda i: (i, 0)),
            const(w_out.shape),
            const(w_pg.shape),
            const(w_pe.shape),
            const((1, D_MODEL)),
            const((1, D_MODEL)),
        ],
        out_specs=pl.BlockSpec((tm, D_MODEL), lambda i: (i, 0)),
        out_shape=jax.ShapeDtypeStruct((t, D_MODEL), F32),
        compiler_params=pltpu.CompilerParams(
            dimension_semantics=("arbitrary",),
            vmem_limit_bytes=V7X_VMEM_LIMIT),
        name="out_stage",
    )(x2, y, p2, w_out, w_pg, w_pe, pe_norm_w, final_norm_w)


def kernel(x, p, norm_w, w_in, conv_w, conv_b, ml_b_i, ml_b_f, ml_norm_w, hg_lb, hg_norm_w,
           w_out, pe_norm_w, w_pg, w_pe, final_norm_w):
    batch, seq, _ = x.shape
    t = batch * seq
    chunk = min(CHUNK, seq)
    x2 = x.reshape(t, D_MODEL).astype(F32)
    p2 = p.reshape(-1, PE_DIM).astype(F32)

    w = w_in[0]
    n_gate = 2 * ML_HEADS
    gate_lo = 2 * ML_QK + 3 * ML_WIDTH
    w_main = jnp.concatenate([w[:, :gate_lo], w[:, gate_lo + n_gate:]], axis=1).astype(BF16)
    w_main = _col_tiles(w_main, TN_IN)
    w_gate = jnp.pad(w[:, gate_lo:gate_lo + n_gate].T, ((0, GATE_ROWS - n_gate), (0, 0))).astype(BF16)
    gbias = jnp.pad(jnp.concatenate([ml_b_i[0], ml_b_f[0]]).astype(F32), (0, GATE_ROWS - n_gate))
    gbias = jnp.broadcast_to(gbias[:, None], (GATE_ROWS, LANES))
    nw = norm_w[0].reshape(1, D_MODEL).astype(F32)

    y = _mixers(x2, nw, w_main, w_gate,
                conv_w[0].astype(F32), conv_b[0].reshape(1, -1).astype(F32), gbias,
                ml_norm_w[0].reshape(1, -1).astype(F32), hg_lb.astype(F32),
                hg_norm_w[0].reshape(1, -1).astype(F32), chunk, seq // chunk)
    out = _out_stage(x2, y, p2, w_out[0].astype(BF16), w_pg[0].astype(BF16), w_pe[0].astype(BF16),
                     pe_norm_w[0].reshape(1, -1).astype(F32),
                     final_norm_w.reshape(1, -1).astype(F32))
    return out.reshape(batch, seq, D_MODEL).astype(x.dtype)
```

```python
import functools
import math

import jax
import jax.numpy as jnp
from jax import lax
from jax.experimental import pallas as pl
from jax.experimental.pallas import tpu as pltpu

F32 = jnp.float32
BF16 = jnp.bfloat16

D_MODEL = 2048
PE_DIM = 256
ML_HEADS = 4
ML_DQK = 128
ML_DV = 256
ML_QK = ML_HEADS * ML_DQK
ML_WIDTH = ML_HEADS * ML_DV
ML_CONV = 4
HG_HEADS = 8
HG_E = 128
HG_DV = 128
HG_WIDTH = HG_HEADS * HG_DV
EPS = 1e-6

MAIN_COLS = 8192
COL_QK = 0
COL_V = 1024
COL_O = 2048
COL_Z = 3072
COL_HQ = 4096
COL_HF = 5120
COL_HI = 6144
COL_HG = 7168
GATE_ROWS = 16

LANES = 128
SUBLANES = 8
NEG_BIG = -1e30
V7X_VMEM_LIMIT = 58 * 1024 * 1024

CHUNK = 128
TN_IN = 256
TM_OUT = 512
OUT_SPLIT = 2


def _dot(a, b):
    return jnp.dot(a, b, preferred_element_type=F32)


def _dot_nt(a, b):
    return lax.dot_general(a, b, (((1,), (1,)), ((), ())), preferred_element_type=F32)


def _dot_tn(a, b):
    return lax.dot_general(a, b, (((0,), (0,)), ((), ())), preferred_element_type=F32)


def _col_tiles(w, tn):
    k, n = w.shape
    return w.reshape(k, n // tn, tn).transpose(1, 0, 2)


def _rms_scale(x):
    return lax.rsqrt(jnp.mean(x * x, axis=-1, keepdims=True) + EPS)


def _scan_lanes(x, op, fill):
    n = x.shape[1]
    lane = lax.broadcasted_iota(jnp.int32, x.shape, 1)
    s = 1
    while s < n:
        shifted = pltpu.roll(x, s, axis=1)
        x = op(x, jnp.where(lane >= s, shifted, fill))
        s *= 2
    return x


def _sigmoid(x):
    return 1.0 / (1.0 + jnp.exp(-x))


def _silu(x):
    return x * _sigmoid(x)


def _normalise(x, nw_ref, u_ref):
    u_ref[...] = (x * _rms_scale(x) * nw_ref[...]).astype(BF16)


def _cols(slab_ref, c0, c1):
    tn = slab_ref.shape[2]
    parts = []
    while c0 < c1:
        t, off = divmod(c0, tn)
        n = min(c1 - c0, tn - off)
        parts.append(slab_ref[t, :, off:off + n])
        c0 += n
    return parts[0] if len(parts) == 1 else jnp.concatenate(parts, axis=1)


def _projection_tasks(u_ref, w_ref, wg_ref, proj_dst, gate_dst):
    n_tiles = w_ref.shape[0]

    def gates():
        gate_dst[...] = _dot_nt(wg_ref[...], u_ref[...])

    def tile(j):
        proj_dst[j] = _dot(u_ref[...], w_ref[j])

    return [gates] + [functools.partial(tile, j) for j in range(n_tiles)]


class _Filler:
    def __init__(self, tasks, slots):
        self._tasks = list(tasks)
        self._total = len(self._tasks)
        self._slots = slots
        self._used = 0

    def __call__(self, n=1):
        self._used += n
        due = -(-self._total * self._used // self._slots)
        while self._total - len(self._tasks) < due and self._tasks:
            self._tasks.pop(0)()

    def flush(self):
        while self._tasks:
            self._tasks.pop(0)()


def _mlstm_gates(gate_ref, gbias_ref, m_ref, L):
    i8 = gate_ref[0:SUBLANES, :] + gbias_ref[0:SUBLANES, 0:1]
    f8 = pltpu.roll(i8, ML_HEADS, axis=0)
    lf8 = jnp.minimum(f8, 0.0) - jnp.log1p(jnp.exp(-jnp.abs(f8)))
    b8 = _scan_lanes(lf8, jnp.add, 0.0)
    a8 = i8 - b8
    cm8 = _scan_lanes(a8, jnp.maximum, NEG_BIG)
    mprev8 = m_ref[:, 0:1]
    mm8 = jnp.maximum(mprev8, cm8)
    mlast8 = jnp.max(mm8, axis=1, keepdims=True)
    g8 = jnp.sum(lf8, axis=1, keepdims=True)
    m_ref[...] = jnp.broadcast_to(g8 + mlast8, m_ref.shape)
    table = jnp.concatenate(
        [mm8, b8, a8,
         jnp.broadcast_to(mlast8, (SUBLANES, L)),
         jnp.broadcast_to(mprev8, (SUBLANES, L)),
         jnp.zeros((LANES - 5 * SUBLANES, L), F32)], axis=0)
    return a8, table.T


def _mlstm_head(h, q_all, k_all, qk_h, a8, ct, causal, proj_ref, mlnw_ref, c_ref, n_ref, y_ref, rows,
                fill):
    mm_c = ct[:, h:h + 1]
    b_c = ct[:, 8 + h:9 + h]
    a_c = ct[:, 16 + h:17 + h]
    mlast_c = ct[:, 24 + h:25 + h]
    mprev_c = ct[:, 32 + h:33 + h]
    a_r = a8[h:h + 1, :]

    qh = q_all[:, h * ML_DQK:(h + 1) * ML_DQK]
    kh = k_all[:, h * ML_DQK:(h + 1) * ML_DQK]
    vb = _cols(proj_ref, COL_V + h * ML_DV, COL_V + (h + 1) * ML_DV).astype(BF16)

    decay = jnp.exp(jnp.where(causal, a_r - mm_c, NEG_BIG))
    scores = qk_h * decay
    w_inter = jnp.exp(mprev_c - mm_c)
    c_old = c_ref[h]
    n_old = n_ref[h]
    num = _dot(scores.astype(BF16), vb) + w_inter * _dot(qh.astype(BF16), c_old.astype(BF16))
    den = (jnp.sum(scores, axis=-1, keepdims=True)
           + w_inter * jnp.sum(qh * n_old[0:1, :], axis=-1, keepdims=True))
    hh = num / jnp.maximum(jnp.abs(den), jnp.exp(-(b_c + mm_c)))
    fill()

    wa_c = jnp.exp(a_c - mlast_c)
    ws_c = jnp.exp(mprev_c - mlast_c)
    kw = kh * wa_c
    c_ref[h] = ws_c[0:ML_DQK, :] * c_old + _dot_tn(kw.astype(BF16), vb)
    n_ref[h] = ws_c[0:SUBLANES, :] * n_old + jnp.sum(kw, axis=0, keepdims=True)

    hn = hh * _rms_scale(hh) * mlnw_ref[:, h * ML_DV:(h + 1) * ML_DV]
    o_gate = _sigmoid(_cols(proj_ref, COL_O + h * ML_DV, COL_O + (h + 1) * ML_DV))
    z = _cols(proj_ref, COL_Z + h * ML_DV, COL_Z + (h + 1) * ML_DV)
    y_ref[rows, h * ML_DV:(h + 1) * ML_DV] = (o_gate * hn * _silu(z)).astype(y_ref.dtype)


def _split_halves(a, m):
    n, w = a.shape
    a4 = a.reshape(n // (2 * m), 2, m, w)
    return a4[:, 0], a4[:, 1]


def _merge_halves(lo, hi):
    nb, m, w = lo.shape
    return jnp.stack([lo, hi], axis=1).reshape(2 * nb * m, w)


def _block_diag(a):
    z = jnp.zeros((a.shape[0], LANES), a.dtype)
    return jnp.concatenate([jnp.concatenate([a[:, :LANES], z], axis=1),
                            jnp.concatenate([z, a[:, LANES:]], axis=1)], axis=0)


def _hgrn2_pair(hp, lb_all, proj_ref, hgnw_ref, st_ref, lv_ref, y_ref, rows, fill, fill_levels):
    L = proj_ref.shape[1]
    nlev = int(math.log2(L))
    width = 2 * HG_E
    sl = slice(hp * width, (hp + 1) * width)
    lb = lb_all[:, sl]
    q = _cols(proj_ref, COL_HQ + hp * width, COL_HQ + (hp + 1) * width)
    v = _cols(proj_ref, COL_HI + hp * width, COL_HI + (hp + 1) * width)
    f_pre = _cols(proj_ref, COL_HF + hp * width, COL_HF + (hp + 1) * width)
    kk = (1.0 - lb) * _sigmoid(-f_pre)
    c = jnp.log(lb + (1.0 - lb) * _sigmoid(f_pre))

    att = None
    q8 = q.reshape(L // SUBLANES, SUBLANES, width)
    kk8 = kk.reshape(L // SUBLANES, SUBLANES, width)
    sub = lax.broadcasted_iota(jnp.int32, q8.shape, 1)
    for lev in range(nlev):
        m = 1 << lev
        if m < SUBLANES:
            c8 = c.reshape(L // SUBLANES, SUBLANES, width)
            upper = (sub & m) != 0
            tot = c8
            s = 1
            while s < m:
                tot = jnp.where((sub & s) != 0, tot, pltpu.roll(tot, SUBLANES - s, axis=1))
                s *= 2
            e = jnp.exp(jnp.where(upper, c8, tot - c8))
            zz = (jnp.where(upper, q8, kk8) * e).reshape(L, width).astype(BF16)
            p = jnp.where(lv_ref[...] == lev, _dot_nt(zz, _block_diag(zz)), 0.0)
            att = p if att is None else att + p
            c = (c8 + jnp.where(upper, pltpu.roll(tot, m, axis=1), 0.0)).reshape(L, width)
        else:
            c_lo, c_hi = _split_halves(c, m)
            tot_lo = jnp.broadcast_to(c_lo[:, m - 1:m, :], c_lo.shape)
            z_lo = _split_halves(kk, m)[0] * jnp.exp(tot_lo - c_lo)
            z_hi = _split_halves(q, m)[1] * jnp.exp(c_hi)
            zz = _merge_halves(z_lo, z_hi).astype(BF16)
            p = _dot_nt(z_hi.reshape(L // 2, width).astype(BF16), _block_diag(zz))
            lv_hi = _split_halves(lv_ref[...], m)[1].reshape(L // 2, 2 * L)
            p = jnp.where(lv_hi == lev, p, 0.0).reshape(L // (2 * m), m, 2 * L)
            att_lo, att_hi = _split_halves(att, m)
            att = _merge_halves(att_lo, att_hi + p)
            c = _merge_halves(c_lo, c_hi + tot_lo)
        if lev in fill_levels:
            fill()

    tot = jnp.broadcast_to(c[L - 1:L, :], c.shape)
    st_old = st_ref[hp]
    vb = v.astype(BF16)
    qk_diag = q * kk
    o = (_dot(att.astype(BF16), _block_diag(vb))
         + _dot_nt((q * jnp.exp(c)).astype(BF16), _block_diag(st_old.astype(BF16))))
    ke = (kk * jnp.exp(tot - c)).astype(BF16)
    upd = jnp.concatenate([_dot_tn(vb[:, :HG_DV], ke[:, :HG_E]),
                           _dot_tn(vb[:, HG_DV:], ke[:, HG_E:])], axis=1)
    st_ref[hp] = st_old * jnp.exp(tot[0:HG_DV, :]) + upd

    gz = _cols(proj_ref, COL_HG + hp * width, COL_HG + (hp + 1) * width)
    gate = _silu(gz)
    nw = hgnw_ref[:, sl]
    for i in range(2):
        hs = slice(i * HG_DV, (i + 1) * HG_DV)
        oi = o[:, hs] + jnp.sum(qk_diag[:, hs], axis=-1, keepdims=True) * v[:, hs]
        on = oi * _rms_scale(oi) * nw[:, hs]
        col = ML_WIDTH + hp * width + i * HG_DV
        y_ref[rows, col:col + HG_DV] = (on * gate[:, hs]).astype(y_ref.dtype)


def _mix(proj_ref, gate_ref, convw_ref, convb_ref, gbias_ref, mlnw_ref, hglb_ref, hgnw_ref,
         y_ref, r0, qkext_ref, c_ref, n_ref, m_ref, st_ref, lv_ref, tasks, late_task):
    L = proj_ref.shape[1]
    rows = slice(r0, r0 + L)
    fill = _Filler(tasks, slots=6 + 5 * (HG_HEADS // 2) + 2 * ML_HEADS)

    fill(2)
    a8, ct = _mlstm_gates(gate_ref, gbias_ref, m_ref, L)
    fill(2)

    qkext_ref[SUBLANES:SUBLANES + L, :] = _cols(proj_ref, COL_QK, COL_QK + 2 * ML_QK)
    conv = convb_ref[...]
    for k in range(ML_CONV):
        off = SUBLANES - (ML_CONV - 1) + k
        conv = conv + qkext_ref[off:off + L, :] * convw_ref[k:k + 1, :]
    qkext_ref[0:SUBLANES, :] = qkext_ref[L:L + SUBLANES, :]
    qk = _silu(conv)
    q_all = qk[:, :ML_QK]
    k_all = qk[:, ML_QK:] * (ML_DQK ** -0.5)
    fill(2)

    lbl = hglb_ref[...]
    lmax = jnp.max(lbl, axis=0, keepdims=True)
    lexp = jnp.exp(lbl - lmax)
    lb_all = lexp[0:1, :] / jnp.sum(lexp, axis=0, keepdims=True)
    for hp in range(HG_HEADS // 2):
        _hgrn2_pair(hp, lb_all, proj_ref, hgnw_ref, st_ref, lv_ref, y_ref, rows, fill,
                    fill_levels=(0, 1, 3, 4, 6))
        if hp == HG_HEADS // 4 - 1:
            late_task()

    rowi = lax.broadcasted_iota(jnp.int32, (L, L), 0)
    coli = lax.broadcasted_iota(jnp.int32, (L, L), 1)
    causal = coli <= rowi
    for h in range(ML_HEADS):
        if h % 2 == 0:
            pair = slice(h * ML_DQK, (h + 2) * ML_DQK)
            qk_pair = _dot_nt(q_all[:, pair].astype(BF16), _block_diag(k_all[:, pair].astype(BF16)))
        qk_h = qk_pair[:, (h % 2) * L:(h % 2 + 1) * L]
        _mlstm_head(h, q_all, k_all, qk_h, a8, ct, causal, proj_ref, mlnw_ref, c_ref, n_ref, y_ref,
                    rows, fill)
        fill()
    fill.flush()


def _mixer_kernel(steps_per_seq,
                  x0_ref, xa_ref, xb_ref, nw_ref, w_ref, wg_ref,
                  convw_ref, convb_ref, gbias_ref, mlnw_ref, hglb_ref, hgnw_ref,
                  y_ref,
                  proj_a, proj_b, gate_a, gate_b, u_a, u_b,
                  qkext_ref, c_ref, n_ref, m_ref, st_ref, lv_ref):
    L = proj_a.shape[1]
    g = pl.program_id(0)

    @pl.when(g == 0)
    def _():
        _normalise(x0_ref[0:L, :], nw_ref, u_a)
        gate_a[...] = _dot_nt(wg_ref[...], u_a[...])

        def project_tile(j, carry):
            proj_a[j] = _dot(u_a[...], w_ref[j])
            return carry

        lax.fori_loop(0, w_ref.shape[0], project_tile, 0)
        _normalise(x0_ref[L:2 * L, :], nw_ref, u_b)
        r = lax.broadcasted_iota(jnp.int32, (L, L), 0)
        c = lax.broadcasted_iota(jnp.int32, (L, L), 1)
        lv = 31 - lax.clz(r ^ c)
        lv = jnp.where(r > c, lv, -1)
        lv_ref[...] = jnp.concatenate([lv, lv], axis=1)

    @pl.when(g % steps_per_seq == 0)
    def _():
        qkext_ref[0:SUBLANES, :] = jnp.zeros((SUBLANES, 2 * ML_QK), F32)
        c_ref[...] = jnp.zeros(c_ref.shape, F32)
        n_ref[...] = jnp.zeros(n_ref.shape, F32)
        m_ref[...] = jnp.zeros(m_ref.shape, F32)
        st_ref[...] = jnp.zeros(st_ref.shape, F32)

    params = (convw_ref, convb_ref, gbias_ref, mlnw_ref, hglb_ref, hgnw_ref)
    state = (qkext_ref, c_ref, n_ref, m_ref, st_ref, lv_ref)
    _mix(proj_a, gate_a, *params, y_ref, 0, *state,
         _projection_tasks(u_b, w_ref, wg_ref, proj_b, gate_b),
         lambda: _normalise(xa_ref[...], nw_ref, u_a))
    _mix(proj_b, gate_b, *params, y_ref, L, *state,
         _projection_tasks(u_a, w_ref, wg_ref, proj_a, gate_a),
         lambda: _normalise(xb_ref[...], nw_ref, u_b))


def _mixers(x2, norm_w, w_main, w_gate, conv_w, conv_b, gbias, ml_norm_w, hg_lb,
            hg_norm_w, chunk, chunks_per_seq):
    t = x2.shape[0]
    L = chunk
    nchunks = t // L
    steps = nchunks // 2
    assert chunks_per_seq % 2 == 0
    const = lambda shape: pl.BlockSpec(shape, lambda g: (0, 0), pipeline_mode=pl.Buffered(1))
    return pl.pallas_call(
        functools.partial(_mixer_kernel, chunks_per_seq // 2),
        grid=(steps,),
        in_specs=[
            const((2 * L, D_MODEL)),
            pl.BlockSpec((L, D_MODEL), lambda g: (jnp.minimum(2 * g + 2, nchunks - 1), 0)),
            pl.BlockSpec((L, D_MODEL), lambda g: (jnp.minimum(2 * g + 3, nchunks - 1), 0)),
            const((1, D_MODEL)),
            pl.BlockSpec(w_main.shape, lambda g: (0, 0, 0), pipeline_mode=pl.Buffered(1)),
            const((GATE_ROWS, D_MODEL)),
            const((ML_CONV, 2 * ML_QK)),
            const((1, 2 * ML_QK)),
            const((GATE_ROWS, LANES)),
            const((1, ML_WIDTH)),
            const((2, HG_WIDTH)),
            const((1, HG_WIDTH)),
        ],
        out_specs=pl.BlockSpec((2 * L, ML_WIDTH + HG_WIDTH), lambda g: (g, 0)),
        out_shape=jax.ShapeDtypeStruct((t, ML_WIDTH + HG_WIDTH), BF16),
        scratch_shapes=[
            pltpu.VMEM((MAIN_COLS // TN_IN, L, TN_IN), F32),
            pltpu.VMEM((MAIN_COLS // TN_IN, L, TN_IN), F32),
            pltpu.VMEM((GATE_ROWS, L), F32),
            pltpu.VMEM((GATE_ROWS, L), F32),
            pltpu.VMEM((L, D_MODEL), BF16),
            pltpu.VMEM((L, D_MODEL), BF16),
            pltpu.VMEM((L + SUBLANES, 2 * ML_QK), F32),
            pltpu.VMEM((ML_HEADS, ML_DQK, ML_DV), F32),
            pltpu.VMEM((ML_HEADS, SUBLANES, ML_DQK), F32),
            pltpu.VMEM((SUBLANES, LANES), F32),
            pltpu.VMEM((HG_HEADS // 2, HG_DV, 2 * HG_E), F32),
            pltpu.VMEM((L, 2 * L), jnp.int32),
        ],
        compiler_params=pltpu.CompilerParams(
            dimension_semantics=("arbitrary",),
            vmem_limit_bytes=V7X_VMEM_LIMIT),
        name="mixers",
    )(x2, x2, x2, norm_w, w_main, w_gate, conv_w, conv_b, gbias, ml_norm_w, hg_lb, hg_norm_w)


def _out_kernel(x_ref, y_ref, p_ref, wout_ref, wpg_ref, wpe_ref, pnw_ref, fnw_ref, o_ref):
    half = x_ref.shape[0] // OUT_SPLIT
    parts = [slice(i * half, (i + 1) * half) for i in range(OUT_SPLIT)]
    hs = [x_ref[rows, :] + _dot(y_ref[rows, :], wout_ref[...]) for rows in parts]
    for rows, h in zip(parts, hs):
        hn = h * _rms_scale(h) * pnw_ref[...]
        gate = _sigmoid(_dot(hn.astype(BF16), wpg_ref[...]))
        emb = _dot(p_ref[rows, :].astype(BF16), wpe_ref[...])
        h = h + gate * emb
        o_ref[rows, :] = h * _rms_scale(h) * fnw_ref[...]


def _out_stage(x2, y, p2, w_out, w_pg, w_pe, pe_norm_w, final_norm_w):
    t = x2.shape[0]
    tm = min(TM_OUT, t)
    const = lambda shape: pl.BlockSpec(shape, lambda i: (0, 0), pipeline_mode=pl.Buffered(1))
    return pl.pallas_call(
        _out_kernel,
        grid=(t // tm,),
        in_specs=[
            pl.BlockSpec((tm, D_MODEL), lambda i: (i, 0)),
            pl.BlockSpec((tm, D_MODEL), lambda i: (i, 0)),
            pl.BlockSpec((tm, PE_DIM), lambda i: (i, 0)),
            const(w_out.shape),
            const(w_pg.shape),
            const(w_pe.shape),
            const((1, D_MODEL)),
            const((1, D_MODEL)),
        ],
        out_specs=pl.BlockSpec((tm, D_MODEL), lambda i: (i, 0)),
        out_shape=jax.ShapeDtypeStruct((t, D_MODEL), F32),
        compiler_params=pltpu.CompilerParams(
            dimension_semantics=("arbitrary",),
            vmem_limit_bytes=V7X_VMEM_LIMIT),
        name="out_stage",
    )(x2, y, p2, w_out, w_pg, w_pe, pe_norm_w, final_norm_w)


def kernel(x, p, norm_w, w_in, conv_w, conv_b, ml_b_i, ml_b_f, ml_norm_w, hg_lb, hg_norm_w,
           w_out, pe_norm_w, w_pg, w_pe, final_norm_w):
    batch, seq, _ = x.shape
    t = batch * seq
    chunk = min(CHUNK, seq)
    x2 = x.reshape(t, D_MODEL).astype(F32)
    p2 = p.reshape(-1, PE_DIM).astype(F32)

    w = w_in[0]
    n_gate = 2 * ML_HEADS
    gate_lo = 2 * ML_QK + 3 * ML_WIDTH
    w_main = jnp.concatenate([_col_tiles(w[:, :gate_lo].astype(BF16), TN_IN),
                              _col_tiles(w[:, gate_lo + n_gate:].astype(BF16), TN_IN)], axis=0)
    w_gate = jnp.pad(w[:, gate_lo:gate_lo + n_gate].T, ((0, GATE_ROWS - n_gate), (0, 0))).astype(BF16)
    gbias = jnp.pad(jnp.concatenate([ml_b_i[0], ml_b_f[0]]).astype(F32), (0, GATE_ROWS - n_gate))
    gbias = jnp.broadcast_to(gbias[:, None], (GATE_ROWS, LANES))
    nw = norm_w[0].reshape(1, D_MODEL).astype(F32)

    y = _mixers(x2, nw, w_main, w_gate,
                conv_w[0].astype(F32), conv_b[0].reshape(1, -1).astype(F32), gbias,
                ml_norm_w[0].reshape(1, -1).astype(F32), hg_lb.astype(F32),
                hg_norm_w[0].reshape(1, -1).astype(F32), chunk, seq // chunk)
    out = _out_stage(x2, y, p2, w_out[0].astype(BF16), w_pg[0].astype(BF16), w_pe[0].astype(BF16),
                     pe_norm_w[0].reshape(1, -1).astype(F32),
                     final_norm_w.reshape(1, -1).astype(F32))
    return out.reshape(batch, seq, D_MODEL).astype(x.dtype)
```

```python
import functools
import math

import jax
import jax.numpy as jnp
from jax import lax
from jax.experimental import pallas as pl
from jax.experimental.pallas import tpu as pltpu

F32 = jnp.float32
BF16 = jnp.bfloat16

D_MODEL = 2048
PE_DIM = 256
ML_HEADS = 4
ML_DQK = 128
ML_DV = 256
ML_QK = ML_HEADS * ML_DQK
ML_WIDTH = ML_HEADS * ML_DV
ML_CONV = 4
HG_HEADS = 8
HG_E = 128
HG_DV = 128
HG_WIDTH = HG_HEADS * HG_DV
EPS = 1e-6

MAIN_COLS = 8192
COL_QK = 0
COL_V = 1024
COL_O = 2048
COL_Z = 3072
COL_HQ = 4096
COL_HF = 5120
COL_HI = 6144
COL_HG = 7168
GATE_ROWS = 16

LANES = 128
SUBLANES = 8
NEG_BIG = -1e30
V7X_VMEM_LIMIT = 58 * 1024 * 1024

CHUNK = 128
TN_IN = 256
TM_OUT = 512
OUT_SPLIT = 2


def _dot(a, b):
    return jnp.dot(a, b, preferred_element_type=F32)


def _dot_nt(a, b):
    return lax.dot_general(a, b, (((1,), (1,)), ((), ())), preferred_element_type=F32)


def _dot_tn(a, b):
    return lax.dot_general(a, b, (((0,), (0,)), ((), ())), preferred_element_type=F32)


def _col_tiles(w, tn):
    k, n = w.shape
    return w.reshape(k, n // tn, tn).transpose(1, 0, 2)


def _rms_scale(x):
    return lax.rsqrt(jnp.mean(x * x, axis=-1, keepdims=True) + EPS)


def _scan_lanes(x, op, fill):
    n = x.shape[1]
    lane = lax.broadcasted_iota(jnp.int32, x.shape, 1)
    s = 1
    while s < n:
        shifted = pltpu.roll(x, s, axis=1)
        x = op(x, jnp.where(lane >= s, shifted, fill))
        s *= 2
    return x


def _sigmoid(x):
    return 1.0 / (1.0 + jnp.exp(-x))


def _silu(x):
    return x * _sigmoid(x)


def _normalise(x, nw_ref, u_ref):
    u_ref[...] = (x * _rms_scale(x) * nw_ref[...]).astype(BF16)


def _cols(slab_ref, c0, c1):
    tn = slab_ref.shape[2]
    parts = []
    while c0 < c1:
        t, off = divmod(c0, tn)
        n = min(c1 - c0, tn - off)
        parts.append(slab_ref[t, :, off:off + n])
        c0 += n
    return parts[0] if len(parts) == 1 else jnp.concatenate(parts, axis=1)


def _projection_tasks(u_ref, w_ref, wg_ref, proj_dst, gate_dst):
    n_tiles = w_ref.shape[0]

    def gates():
        gate_dst[...] = _dot_nt(wg_ref[...], u_ref[...])

    def tile(j):
        proj_dst[j] = _dot(u_ref[...], w_ref[j])

    return [gates] + [functools.partial(tile, j) for j in range(n_tiles)]


class _Filler:
    def __init__(self, tasks, slots):
        self._tasks = list(tasks)
        self._total = len(self._tasks)
        self._slots = slots
        self._used = 0

    def __call__(self, n=1):
        self._used += n
        due = -(-self._total * self._used // self._slots)
        while self._total - len(self._tasks) < due and self._tasks:
            self._tasks.pop(0)()

    def flush(self):
        while self._tasks:
            self._tasks.pop(0)()


def _mlstm_gates(gate_ref, gbias_ref, m_ref, L):
    i8 = gate_ref[0:SUBLANES, :] + gbias_ref[0:SUBLANES, 0:1]
    f8 = pltpu.roll(i8, ML_HEADS, axis=0)
    lf8 = jnp.minimum(f8, 0.0) - jnp.log1p(jnp.exp(-jnp.abs(f8)))
    b8 = _scan_lanes(lf8, jnp.add, 0.0)
    a8 = i8 - b8
    cm8 = _scan_lanes(a8, jnp.maximum, NEG_BIG)
    mprev8 = m_ref[:, 0:1]
    mm8 = jnp.maximum(mprev8, cm8)
    mlast8 = jnp.max(mm8, axis=1, keepdims=True)
    g8 = jnp.sum(lf8, axis=1, keepdims=True)
    m_ref[...] = jnp.broadcast_to(g8 + mlast8, m_ref.shape)
    table = jnp.concatenate(
        [mm8, b8, a8,
         jnp.broadcast_to(mlast8, (SUBLANES, L)),
         jnp.broadcast_to(mprev8, (SUBLANES, L)),
         jnp.zeros((LANES - 5 * SUBLANES, L), F32)], axis=0)
    return a8, table.T


def _mlstm_head(h, q_all, k_all, qk_h, a8, ct, causal, proj_ref, mlnw_ref, c_ref, n_ref, y_ref, rows,
                fill):
    mm_c = ct[:, h:h + 1]
    b_c = ct[:, 8 + h:9 + h]
    a_c = ct[:, 16 + h:17 + h]
    mlast_c = ct[:, 24 + h:25 + h]
    mprev_c = ct[:, 32 + h:33 + h]
    a_r = a8[h:h + 1, :]

    qh = q_all[:, h * ML_DQK:(h + 1) * ML_DQK]
    kh = k_all[:, h * ML_DQK:(h + 1) * ML_DQK]
    vb = _cols(proj_ref, COL_V + h * ML_DV, COL_V + (h + 1) * ML_DV).astype(BF16)

    decay = jnp.exp(jnp.where(causal, a_r - mm_c, NEG_BIG))
    scores = qk_h * decay
    w_inter = jnp.exp(mprev_c - mm_c)
    c_old = c_ref[h]
    n_old = n_ref[h]
    num = _dot(scores.astype(BF16), vb) + w_inter * _dot(qh.astype(BF16), c_old.astype(BF16))
    den = (jnp.sum(scores, axis=-1, keepdims=True)
           + w_inter * jnp.sum(qh * n_old[0:1, :], axis=-1, keepdims=True))
    hh = num / jnp.maximum(jnp.abs(den), jnp.exp(-(b_c + mm_c)))
    fill()

    wa_c = jnp.exp(a_c - mlast_c)
    ws_c = jnp.exp(mprev_c - mlast_c)
    kw = kh * wa_c
    c_ref[h] = ws_c[0:ML_DQK, :] * c_old + _dot_tn(kw.astype(BF16), vb)
    n_ref[h] = ws_c[0:SUBLANES, :] * n_old + jnp.sum(kw, axis=0, keepdims=True)

    hn = hh * _rms_scale(hh) * mlnw_ref[:, h * ML_DV:(h + 1) * ML_DV]
    o_gate = _sigmoid(_cols(proj_ref, COL_O + h * ML_DV, COL_O + (h + 1) * ML_DV))
    z = _cols(proj_ref, COL_Z + h * ML_DV, COL_Z + (h + 1) * ML_DV)
    y_ref[rows, h * ML_DV:(h + 1) * ML_DV] = (o_gate * hn * _silu(z)).astype(y_ref.dtype)


def _split_halves(a, m):
    n, w = a.shape
    a4 = a.reshape(n // (2 * m), 2, m, w)
    return a4[:, 0], a4[:, 1]


def _merge_halves(lo, hi):
    nb, m, w = lo.shape
    return jnp.stack([lo, hi], axis=1).reshape(2 * nb * m, w)


def _block_diag(a):
    z = jnp.zeros((a.shape[0], LANES), a.dtype)
    return jnp.concatenate([jnp.concatenate([a[:, :LANES], z], axis=1),
                            jnp.concatenate([z, a[:, LANES:]], axis=1)], axis=0)


def _hgrn2_pair(hp, lb_all, proj_ref, hgnw_ref, st_ref, lv_ref, y_ref, rows, fill, fill_levels):
    L = proj_ref.shape[1]
    nlev = int(math.log2(L))
    width = 2 * HG_E
    sl = slice(hp * width, (hp + 1) * width)
    lb = lb_all[:, sl]
    q = _cols(proj_ref, COL_HQ + hp * width, COL_HQ + (hp + 1) * width)
    v = _cols(proj_ref, COL_HI + hp * width, COL_HI + (hp + 1) * width)
    f_pre = _cols(proj_ref, COL_HF + hp * width, COL_HF + (hp + 1) * width)
    kk = (1.0 - lb) * _sigmoid(-f_pre)
    c = jnp.log(lb + (1.0 - lb) * _sigmoid(f_pre))

    att = None
    q8 = q.reshape(L // SUBLANES, SUBLANES, width)
    kk8 = kk.reshape(L // SUBLANES, SUBLANES, width)
    sub = lax.broadcasted_iota(jnp.int32, q8.shape, 1)
    for lev in range(nlev):
        m = 1 << lev
        if m < SUBLANES:
            c8 = c.reshape(L // SUBLANES, SUBLANES, width)
            upper = (sub & m) != 0
            tot = c8
            s = 1
            while s < m:
                tot = jnp.where((sub & s) != 0, tot, pltpu.roll(tot, SUBLANES - s, axis=1))
                s *= 2
            e = jnp.exp(jnp.where(upper, c8, tot - c8))
            zz = (jnp.where(upper, q8, kk8) * e).reshape(L, width).astype(BF16)
            p = jnp.where(lv_ref[...] == lev, _dot_nt(zz, _block_diag(zz)), 0.0)
            att = p if att is None else att + p
            c = (c8 + jnp.where(upper, pltpu.roll(tot, m, axis=1), 0.0)).reshape(L, width)
        else:
            c_lo, c_hi = _split_halves(c, m)
            tot_lo = jnp.broadcast_to(c_lo[:, m - 1:m, :], c_lo.shape)
            z_lo = _split_halves(kk, m)[0] * jnp.exp(tot_lo - c_lo)
            z_hi = _split_halves(q, m)[1] * jnp.exp(c_hi)
            zz = _merge_halves(z_lo, z_hi).astype(BF16)
            p = _dot_nt(z_hi.reshape(L // 2, width).astype(BF16), _block_diag(zz))
            lv_hi = _split_halves(lv_ref[...], m)[1].reshape(L // 2, 2 * L)
            p = jnp.where(lv_hi == lev, p, 0.0).reshape(L // (2 * m), m, 2 * L)
            att_lo, att_hi = _split_halves(att, m)
            att = _merge_halves(att_lo, att_hi + p)
            c = _merge_halves(c_lo, c_hi + tot_lo)
        if lev in fill_levels:
            fill()

    tot = jnp.broadcast_to(c[L - 1:L, :], c.shape)
    st_old = st_ref[hp]
    vb = v.astype(BF16)
    qk_diag = q * kk
    o = (_dot(att.astype(BF16), _block_diag(vb))
         + _dot_nt((q * jnp.exp(c)).astype(BF16), _block_diag(st_old.astype(BF16))))
    ke = (kk * jnp.exp(tot - c)).astype(BF16)
    upd = jnp.concatenate([_dot_tn(vb[:, :HG_DV], ke[:, :HG_E]),
                           _dot_tn(vb[:, HG_DV:], ke[:, HG_E:])], axis=1)
    st_ref[hp] = st_old * jnp.exp(tot[0:HG_DV, :]) + upd

    gz = _cols(proj_ref, COL_HG + hp * width, COL_HG + (hp + 1) * width)
    gate = _silu(gz)
    nw = hgnw_ref[:, sl]
    for i in range(2):
        hs = slice(i * HG_DV, (i + 1) * HG_DV)
        oi = o[:, hs] + jnp.sum(qk_diag[:, hs], axis=-1, keepdims=True) * v[:, hs]
        on = oi * _rms_scale(oi) * nw[:, hs]
        col = ML_WIDTH + hp * width + i * HG_DV
        y_ref[rows, col:col + HG_DV] = (on * gate[:, hs]).astype(y_ref.dtype)


def _mix(proj_ref, gate_ref, convw_ref, convb_ref, gbias_ref, mlnw_ref, hglb_ref, hgnw_ref,
         y_ref, r0, qkext_ref, c_ref, n_ref, m_ref, st_ref, lv_ref, tasks, late_task):
    L = proj_ref.shape[1]
    rows = slice(r0, r0 + L)
    fill = _Filler(tasks, slots=6 + 5 * (HG_HEADS // 2) + 2 * ML_HEADS)

    fill(2)
    a8, ct = _mlstm_gates(gate_ref, gbias_ref, m_ref, L)
    fill(2)

    qkext_ref[SUBLANES:SUBLANES + L, :] = _cols(proj_ref, COL_QK, COL_QK + 2 * ML_QK)
    conv = convb_ref[...]
    for k in range(ML_CONV):
        off = SUBLANES - (ML_CONV - 1) + k
        conv = conv + qkext_ref[off:off + L, :] * convw_ref[k:k + 1, :]
    qkext_ref[0:SUBLANES, :] = qkext_ref[L:L + SUBLANES, :]
    qk = _silu(conv)
    q_all = qk[:, :ML_QK]
    k_all = qk[:, ML_QK:] * (ML_DQK ** -0.5)
    fill(2)

    lbl = hglb_ref[...]
    lmax = jnp.max(lbl, axis=0, keepdims=True)
    lexp = jnp.exp(lbl - lmax)
    lb_all = lexp[0:1, :] / jnp.sum(lexp, axis=0, keepdims=True)
    for hp in range(HG_HEADS // 2):
        _hgrn2_pair(hp, lb_all, proj_ref, hgnw_ref, st_ref, lv_ref, y_ref, rows, fill,
                    fill_levels=(0, 1, 3, 4, 6))
        if hp == HG_HEADS // 4 - 1:
            late_task()

    rowi = lax.broadcasted_iota(jnp.int32, (L, L), 0)
    coli = lax.broadcasted_iota(jnp.int32, (L, L), 1)
    causal = coli <= rowi
    for h in range(ML_HEADS):
        if h % 2 == 0:
            pair = slice(h * ML_DQK, (h + 2) * ML_DQK)
            qk_pair = _dot_nt(q_all[:, pair].astype(BF16), _block_diag(k_all[:, pair].astype(BF16)))
        qk_h = qk_pair[:, (h % 2) * L:(h % 2 + 1) * L]
        _mlstm_head(h, q_all, k_all, qk_h, a8, ct, causal, proj_ref, mlnw_ref, c_ref, n_ref, y_ref,
                    rows, fill)
        fill()
    fill.flush()


def _mixer_kernel(steps_per_seq,
                  x0_ref, xn_ref, nw_ref, w_ref, wg_ref,
                  convw_ref, convb_ref, gbias_ref, mlnw_ref, hglb_ref, hgnw_ref,
                  y_ref,
                  proj_s, gate_s, u_s,
                  qkext_ref, c_ref, n_ref, m_ref, st_ref, lv_ref):
    L = proj_s.shape[2]
    g = pl.program_id(0)

    @pl.when(g == 0)
    def _():
        _normalise(x0_ref[0:L, :], nw_ref, u_s.at[0])
        gate_s[0] = _dot_nt(wg_ref[...], u_s[0])

        def project_tile(j, carry):
            proj_s[0, j] = _dot(u_s[0], w_ref[j])
            return carry

        lax.fori_loop(0, w_ref.shape[0], project_tile, 0)
        _normalise(x0_ref[L:2 * L, :], nw_ref, u_s.at[1])
        r = lax.broadcasted_iota(jnp.int32, (L, L), 0)
        c = lax.broadcasted_iota(jnp.int32, (L, L), 1)
        lv = 31 - lax.clz(r ^ c)
        lv = jnp.where(r > c, lv, -1)
        lv_ref[...] = jnp.concatenate([lv, lv], axis=1)

    @pl.when(g % steps_per_seq == 0)
    def _():
        qkext_ref[0:SUBLANES, :] = jnp.zeros((SUBLANES, 2 * ML_QK), F32)
        c_ref[...] = jnp.zeros(c_ref.shape, F32)
        n_ref[...] = jnp.zeros(n_ref.shape, F32)
        m_ref[...] = jnp.zeros(m_ref.shape, F32)
        st_ref[...] = jnp.zeros(st_ref.shape, F32)

    params = (convw_ref, convb_ref, gbias_ref, mlnw_ref, hglb_ref, hgnw_ref)
    state = (qkext_ref, c_ref, n_ref, m_ref, st_ref, lv_ref)
    cur = g % 2
    nxt = 1 - cur
    _mix(proj_s.at[cur], gate_s.at[cur], *params, y_ref, 0, *state,
         _projection_tasks(u_s.at[nxt], w_ref, wg_ref, proj_s.at[nxt], gate_s.at[nxt]),
         lambda: _normalise(xn_ref[...], nw_ref, u_s.at[cur]))


def _mixers(x2, norm_w, w_main, w_gate, conv_w, conv_b, gbias, ml_norm_w, hg_lb,
            hg_norm_w, chunk, chunks_per_seq):
    t = x2.shape[0]
    L = chunk
    nchunks = t // L
    const = lambda shape: pl.BlockSpec(shape, lambda g: (0, 0), pipeline_mode=pl.Buffered(1))
    return pl.pallas_call(
        functools.partial(_mixer_kernel, chunks_per_seq),
        grid=(nchunks,),
        in_specs=[
            const((2 * L, D_MODEL)),
            pl.BlockSpec((L, D_MODEL), lambda g: (jnp.minimum(g + 2, nchunks - 1), 0)),
            const((1, D_MODEL)),
            pl.BlockSpec(w_main.shape, lambda g: (0, 0, 0), pipeline_mode=pl.Buffered(1)),
            const((GATE_ROWS, D_MODEL)),
            const((ML_CONV, 2 * ML_QK)),
            const((1, 2 * ML_QK)),
            const((GATE_ROWS, LANES)),
            const((1, ML_WIDTH)),
            const((2, HG_WIDTH)),
            const((1, HG_WIDTH)),
        ],
        out_specs=pl.BlockSpec((L, ML_WIDTH + HG_WIDTH), lambda g: (g, 0)),
        out_shape=jax.ShapeDtypeStruct((t, ML_WIDTH + HG_WIDTH), BF16),
        scratch_shapes=[
            pltpu.VMEM((2, MAIN_COLS // TN_IN, L, TN_IN), F32),
            pltpu.VMEM((2, GATE_ROWS, L), F32),
            pltpu.VMEM((2, L, D_MODEL), BF16),
            pltpu.VMEM((L + SUBLANES, 2 * ML_QK), F32),
            pltpu.VMEM((ML_HEADS, ML_DQK, ML_DV), F32),
            pltpu.VMEM((ML_HEADS, SUBLANES, ML_DQK), F32),
            pltpu.VMEM((SUBLANES, LANES), F32),
            pltpu.VMEM((HG_HEADS // 2, HG_DV, 2 * HG_E), F32),
            pltpu.VMEM((L, 2 * L), jnp.int32),
        ],
        compiler_params=pltpu.CompilerParams(
            dimension_semantics=("arbitrary",),
            vmem_limit_bytes=V7X_VMEM_LIMIT),
        name="mixers",
    )(x2, x2, norm_w, w_main, w_gate, conv_w, conv_b, gbias, ml_norm_w, hg_lb, hg_norm_w)


def _out_kernel(x_ref, y_ref, p_ref, wout_ref, wpg_ref, wpe_ref, pnw_ref, fnw_ref, o_ref):
    half = x_ref.shape[0] // OUT_SPLIT
    parts = [slice(i * half, (i + 1) * half) for i in range(OUT_SPLIT)]
    hs = [x_ref[rows, :] + _dot(y_ref[rows, :], wout_ref[...]) for rows in parts]
    for rows, h in zip(parts, hs):
        hn = h * _rms_scale(h) * pnw_ref[...]
        gate = _sigmoid(_dot(hn.astype(BF16), wpg_ref[...]))
        emb = _dot(p_ref[rows, :].astype(BF16), wpe_ref[...])
        h = h + gate * emb
        o_ref[rows, :] = h * _rms_scale(h) * fnw_ref[...]


def _out_stage(x2, y, p2, w_out, w_pg, w_pe, pe_norm_w, final_norm_w):
    t = x2.shape[0]
    tm = min(TM_OUT, t)
    const = lambda shape: pl.BlockSpec(shape, lambda i: (0, 0), pipeline_mode=pl.Buffered(1))
    return pl.pallas_call(
        _out_kernel,
        grid=(t // tm,),
        in_specs=[
            pl.BlockSpec((tm, D_MODEL), lambda i: (i, 0)),
            pl.BlockSpec((tm, D_MODEL), lambda i: (i, 0)),
            pl.BlockSpec((tm, PE_DIM), lambda i: (i, 0)),
            const(w_out.shape),
            const(w_pg.shape),
            const(w_pe.shape),
            const((1, D_MODEL)),
            const((1, D_MODEL)),
        ],
        out_specs=pl.BlockSpec((tm, D_MODEL), lambda i: (i, 0)),
        out_shape=jax.ShapeDtypeStruct((t, D_MODEL), F32),
        compiler_params=pltpu.CompilerParams(
            dimension_semantics=("arbitrary",),
            vmem_limit_bytes=V7X_VMEM_LIMIT),
        name="out_stage",
    )(x2, y, p2, w_out, w_pg, w_pe, pe_norm_w, final_norm_w)


def kernel(x, p, norm_w, w_in, conv_w, conv_b, ml_b_i, ml_b_f, ml_norm_w, hg_lb, hg_norm_w,
           w_out, pe_norm_w, w_pg, w_pe, final_norm_w):
    batch, seq, _ = x.shape
    t = batch * seq
    chunk = min(CHUNK, seq)
    x2 = x.reshape(t, D_MODEL).astype(F32)
    p2 = p.reshape(-1, PE_DIM).astype(F32)

    w = w_in[0]
    n_gate = 2 * ML_HEADS
    gate_lo = 2 * ML_QK + 3 * ML_WIDTH
    w_main = jnp.concatenate([w[:, :gate_lo], w[:, gate_lo + n_gate:]], axis=1).astype(BF16)
    w_main = _col_tiles(w_main, TN_IN)
    w_gate = jnp.pad(w[:, gate_lo:gate_lo + n_gate].T, ((0, GATE_ROWS - n_gate), (0, 0))).astype(BF16)
    gbias = jnp.pad(jnp.concatenate([ml_b_i[0], ml_b_f[0]]).astype(F32), (0, GATE_ROWS - n_gate))
    gbias = jnp.broadcast_to(gbias[:, None], (GATE_ROWS, LANES))
    nw = norm_w[0].reshape(1, D_MODEL).astype(F32)

    y = _mixers(x2, nw, w_main, w_gate,
                conv_w[0].astype(F32), conv_b[0].reshape(1, -1).astype(F32), gbias,
                ml_norm_w[0].reshape(1, -1).astype(F32), hg_lb.astype(F32),
                hg_norm_w[0].reshape(1, -1).astype(F32), chunk, seq // chunk)
    out = _out_stage(x2, y, p2, w_out[0].astype(BF16), w_pg[0].astype(BF16), w_pe[0].astype(BF16),
                     pe_norm_w[0].reshape(1, -1).astype(F32),
                     final_norm_w.reshape(1, -1).astype(F32))
    return out.reshape(batch, seq, D_MODEL).astype(x.dtype)
```

```python
import functools
import math

import jax
import jax.numpy as jnp
from jax import lax
from jax.experimental import pallas as pl
from jax.experimental.pallas import tpu as pltpu

F32 = jnp.float32
BF16 = jnp.bfloat16

D_MODEL = 2048
PE_DIM = 256
ML_HEADS = 4
ML_DQK = 128
ML_DV = 256
ML_QK = ML_HEADS * ML_DQK
ML_WIDTH = ML_HEADS * ML_DV
ML_CONV = 4
HG_HEADS = 8
HG_E = 128
HG_DV = 128
HG_WIDTH = HG_HEADS * HG_DV
EPS = 1e-6

MAIN_COLS = 8192
COL_QK = 0
COL_V = 1024
COL_O = 2048
COL_Z = 3072
COL_HQ = 4096
COL_HF = 5120
COL_HI = 6144
COL_HG = 7168
GATE_ROWS = 16

LANES = 128
SUBLANES = 8
NEG_BIG = -1e30
V7X_VMEM_LIMIT = 58 * 1024 * 1024

CHUNK = 128
TN_IN = 256
TM_OUT = 512
OUT_SPLIT = 2


def _dot(a, b):
    return jnp.dot(a, b, preferred_element_type=F32)


def _dot_nt(a, b):
    return lax.dot_general(a, b, (((1,), (1,)), ((), ())), preferred_element_type=F32)


def _dot_tn(a, b):
    return lax.dot_general(a, b, (((0,), (0,)), ((), ())), preferred_element_type=F32)


def _col_tiles(w, tn):
    k, n = w.shape
    return w.reshape(k, n // tn, tn).transpose(1, 0, 2)


def _rms_scale(x):
    return lax.rsqrt(jnp.mean(x * x, axis=-1, keepdims=True) + EPS)


def _scan_lanes(x, op, fill):
    n = x.shape[1]
    lane = lax.broadcasted_iota(jnp.int32, x.shape, 1)
    s = 1
    while s < n:
        shifted = pltpu.roll(x, s, axis=1)
        x = op(x, jnp.where(lane >= s, shifted, fill))
        s *= 2
    return x


def _sigmoid(x):
    return 1.0 / (1.0 + jnp.exp(-x))


def _silu(x):
    return x * _sigmoid(x)


def _normalise(x, nw_ref, u_ref):
    u_ref[...] = (x * _rms_scale(x) * nw_ref[...]).astype(BF16)


def _cols(slab_ref, c0, c1):
    tn = slab_ref.shape[2]
    parts = []
    while c0 < c1:
        t, off = divmod(c0, tn)
        n = min(c1 - c0, tn - off)
        parts.append(slab_ref[t, :, off:off + n])
        c0 += n
    return parts[0] if len(parts) == 1 else jnp.concatenate(parts, axis=1)


def _projection_tasks(u_ref, w_ref, wg_ref, proj_dst, gate_dst):
    n_tiles = w_ref.shape[0]

    def gates():
        gate_dst[...] = _dot_nt(wg_ref[...], u_ref[...])

    def tile(j):
        proj_dst[j] = _dot(u_ref[...], w_ref[j])

    return [gates] + [functools.partial(tile, j) for j in range(n_tiles)]


class _Filler:
    def __init__(self, tasks, slots):
        self._tasks = list(tasks)
        self._total = len(self._tasks)
        self._slots = slots
        self._used = 0

    def __call__(self, n=1):
        self._used += n
        due = -(-self._total * self._used // self._slots)
        while self._total - len(self._tasks) < due and self._tasks:
            self._tasks.pop(0)()

    def flush(self):
        while self._tasks:
            self._tasks.pop(0)()


def _mlstm_gates(gate_ref, gbias_ref, m_ref, L):
    i8 = gate_ref[0:SUBLANES, :] + gbias_ref[0:SUBLANES, 0:1]
    f8 = pltpu.roll(i8, ML_HEADS, axis=0)
    lf8 = jnp.minimum(f8, 0.0) - jnp.log1p(jnp.exp(-jnp.abs(f8)))
    b8 = _scan_lanes(lf8, jnp.add, 0.0)
    a8 = i8 - b8
    cm8 = _scan_lanes(a8, jnp.maximum, NEG_BIG)
    mprev8 = m_ref[:, 0:1]
    mm8 = jnp.maximum(mprev8, cm8)
    mlast8 = jnp.max(mm8, axis=1, keepdims=True)
    g8 = jnp.sum(lf8, axis=1, keepdims=True)
    m_ref[...] = jnp.broadcast_to(g8 + mlast8, m_ref.shape)
    table = jnp.concatenate(
        [mm8, b8, a8,
         jnp.broadcast_to(mlast8, (SUBLANES, L)),
         jnp.broadcast_to(mprev8, (SUBLANES, L)),
         jnp.zeros((LANES - 5 * SUBLANES, L), F32)], axis=0)
    return a8, table.T


def _mlstm_head(h, q_all, k_all, qk_h, a8, ct, causal, proj_ref, mlnw_ref, c_ref, n_ref, y_ref, rows,
                fill):
    mm_c = ct[:, h:h + 1]
    b_c = ct[:, 8 + h:9 + h]
    a_c = ct[:, 16 + h:17 + h]
    mlast_c = ct[:, 24 + h:25 + h]
    mprev_c = ct[:, 32 + h:33 + h]
    a_r = a8[h:h + 1, :]

    qh = q_all[:, h * ML_DQK:(h + 1) * ML_DQK]
    kh = k_all[:, h * ML_DQK:(h + 1) * ML_DQK]
    vb = _cols(proj_ref, COL_V + h * ML_DV, COL_V + (h + 1) * ML_DV).astype(BF16)

    decay = jnp.exp(jnp.where(causal, a_r - mm_c, NEG_BIG))
    scores = qk_h * decay
    w_inter = jnp.exp(mprev_c - mm_c)
    c_old = c_ref[h]
    n_old = n_ref[h]
    num = _dot(scores.astype(BF16), vb) + w_inter * _dot(qh.astype(BF16), c_old.astype(BF16))
    den = (jnp.sum(scores, axis=-1, keepdims=True)
           + w_inter * jnp.sum(qh * n_old[0:1, :], axis=-1, keepdims=True))
    hh = num / jnp.maximum(jnp.abs(den), jnp.exp(-(b_c + mm_c)))
    fill()

    wa_c = jnp.exp(a_c - mlast_c)
    ws_c = jnp.exp(mprev_c - mlast_c)
    kw = kh * wa_c
    c_ref[h] = ws_c[0:ML_DQK, :] * c_old + _dot_tn(kw.astype(BF16), vb)
    n_ref[h] = ws_c[0:SUBLANES, :] * n_old + jnp.sum(kw, axis=0, keepdims=True)

    hn = hh * _rms_scale(hh) * mlnw_ref[:, h * ML_DV:(h + 1) * ML_DV]
    o_gate = _sigmoid(_cols(proj_ref, COL_O + h * ML_DV, COL_O + (h + 1) * ML_DV))
    z = _cols(proj_ref, COL_Z + h * ML_DV, COL_Z + (h + 1) * ML_DV)
    y_ref[rows, h * ML_DV:(h + 1) * ML_DV] = (o_gate * hn * _silu(z)).astype(y_ref.dtype)


def _split_halves(a, m):
    n, w = a.shape
    a4 = a.reshape(n // (2 * m), 2, m, w)
    return a4[:, 0], a4[:, 1]


def _merge_halves(lo, hi):
    nb, m, w = lo.shape
    return jnp.stack([lo, hi], axis=1).reshape(2 * nb * m, w)


def _block_diag(a):
    z = jnp.zeros((a.shape[0], LANES), a.dtype)
    return jnp.concatenate([jnp.concatenate([a[:, :LANES], z], axis=1),
                            jnp.concatenate([z, a[:, LANES:]], axis=1)], axis=0)


def _hgrn2_pair(hp, lb_all, proj_ref, hgnw_ref, st_ref, lv_ref, y_ref, rows, fill, fill_levels):
    L = proj_ref.shape[1]
    nlev = int(math.log2(L))
    width = 2 * HG_E
    sl = slice(hp * width, (hp + 1) * width)
    lb = lb_all[:, sl]
    q = _cols(proj_ref, COL_HQ + hp * width, COL_HQ + (hp + 1) * width)
    v = _cols(proj_ref, COL_HI + hp * width, COL_HI + (hp + 1) * width)
    f_pre = _cols(proj_ref, COL_HF + hp * width, COL_HF + (hp + 1) * width)
    kk = (1.0 - lb) * _sigmoid(-f_pre)
    c = jnp.log(lb + (1.0 - lb) * _sigmoid(f_pre))

    att = None
    q8 = q.reshape(L // SUBLANES, SUBLANES, width)
    kk8 = kk.reshape(L // SUBLANES, SUBLANES, width)
    sub = lax.broadcasted_iota(jnp.int32, q8.shape, 1)
    for lev in range(nlev):
        m = 1 << lev
        if m < SUBLANES:
            c8 = c.reshape(L // SUBLANES, SUBLANES, width)
            upper = (sub & m) != 0
            tot = c8
            s = 1
            while s < m:
                tot = jnp.where((sub & s) != 0, tot, pltpu.roll(tot, SUBLANES - s, axis=1))
                s *= 2
            e = jnp.exp(jnp.where(upper, c8, tot - c8))
            zz = (jnp.where(upper, q8, kk8) * e).reshape(L, width).astype(BF16)
            p = jnp.where(lv_ref[...] == lev, _dot_nt(zz, _block_diag(zz)), 0.0)
            att = p if att is None else att + p
            c = (c8 + jnp.where(upper, pltpu.roll(tot, m, axis=1), 0.0)).reshape(L, width)
        else:
            c_lo, c_hi = _split_halves(c, m)
            tot_lo = jnp.broadcast_to(c_lo[:, m - 1:m, :], c_lo.shape)
            z_lo = _split_halves(kk, m)[0] * jnp.exp(tot_lo - c_lo)
            z_hi = _split_halves(q, m)[1] * jnp.exp(c_hi)
            zz = _merge_halves(z_lo, z_hi).astype(BF16)
            p = _dot_nt(z_hi.reshape(L // 2, width).astype(BF16), _block_diag(zz))
            lv_hi = _split_halves(lv_ref[...], m)[1].reshape(L // 2, 2 * L)
            p = jnp.where(lv_hi == lev, p, 0.0).reshape(L // (2 * m), m, 2 * L)
            att_lo, att_hi = _split_halves(att, m)
            att = _merge_halves(att_lo, att_hi + p)
            c = _merge_halves(c_lo, c_hi + tot_lo)
        if lev in fill_levels:
            fill()

    tot = jnp.broadcast_to(c[L - 1:L, :], c.shape)
    st_old = st_ref[hp]
    vb = v.astype(BF16)
    qk_diag = q * kk
    o = (_dot(att.astype(BF16), _block_diag(vb))
         + _dot_nt((q * jnp.exp(c)).astype(BF16), _block_diag(st_old.astype(BF16))))
    ke = (kk * jnp.exp(tot - c)).astype(BF16)
    upd = jnp.concatenate([_dot_tn(vb[:, :HG_DV], ke[:, :HG_E]),
                           _dot_tn(vb[:, HG_DV:], ke[:, HG_E:])], axis=1)
    st_ref[hp] = st_old * jnp.exp(tot[0:HG_DV, :]) + upd

    gz = _cols(proj_ref, COL_HG + hp * width, COL_HG + (hp + 1) * width)
    gate = _silu(gz)
    nw = hgnw_ref[:, sl]
    for i in range(2):
        hs = slice(i * HG_DV, (i + 1) * HG_DV)
        oi = o[:, hs] + jnp.sum(qk_diag[:, hs], axis=-1, keepdims=True) * v[:, hs]
        on = oi * _rms_scale(oi) * nw[:, hs]
        col = ML_WIDTH + hp * width + i * HG_DV
        y_ref[rows, col:col + HG_DV] = (on * gate[:, hs]).astype(y_ref.dtype)


def _mix(proj_ref, gate_ref, convw_ref, convb_ref, gbias_ref, mlnw_ref, hglb_ref, hgnw_ref,
         y_ref, r0, qkext_ref, c_ref, n_ref, m_ref, st_ref, lv_ref, tasks, late_task):
    L = proj_ref.shape[1]
    rows = slice(r0, r0 + L)
    fill = _Filler(tasks, slots=6 + 5 * (HG_HEADS // 2) + 2 * ML_HEADS)

    fill(2)
    a8, ct = _mlstm_gates(gate_ref, gbias_ref, m_ref, L)
    fill(2)

    cur = _cols(proj_ref, COL_QK, COL_QK + 2 * ML_QK)
    ext8 = jnp.concatenate([qkext_ref[0:SUBLANES, :], cur], axis=0).reshape(
        L // SUBLANES + 1, SUBLANES, 2 * ML_QK)
    srow = lax.broadcasted_iota(jnp.int32, (L // SUBLANES, SUBLANES, 2 * ML_QK), 1)
    conv = convb_ref[...] + ext8[1:] * convw_ref[ML_CONV - 1:ML_CONV, :]
    for d in range(1, ML_CONV):
        rot = pltpu.roll(ext8, d, axis=1)
        tap = ML_CONV - 1 - d
        conv = conv + jnp.where(srow >= d, rot[1:], rot[:-1]) * convw_ref[tap:tap + 1, :]
    conv = conv.reshape(L, 2 * ML_QK)
    qkext_ref[0:SUBLANES, :] = cur[L - SUBLANES:L, :]
    qk = _silu(conv)
    q_all = qk[:, :ML_QK]
    k_all = qk[:, ML_QK:] * (ML_DQK ** -0.5)
    fill(2)

    lbl = hglb_ref[...]
    lmax = jnp.max(lbl, axis=0, keepdims=True)
    lexp = jnp.exp(lbl - lmax)
    lb_all = lexp[0:1, :] / jnp.sum(lexp, axis=0, keepdims=True)
    for hp in range(HG_HEADS // 2):
        _hgrn2_pair(hp, lb_all, proj_ref, hgnw_ref, st_ref, lv_ref, y_ref, rows, fill,
                    fill_levels=(0, 1, 3, 4, 6))
        if hp == HG_HEADS // 4 - 1:
            late_task()

    rowi = lax.broadcasted_iota(jnp.int32, (L, L), 0)
    coli = lax.broadcasted_iota(jnp.int32, (L, L), 1)
    causal = coli <= rowi
    for h in range(ML_HEADS):
        if h % 2 == 0:
            pair = slice(h * ML_DQK, (h + 2) * ML_DQK)
            qk_pair = _dot_nt(q_all[:, pair].astype(BF16), _block_diag(k_all[:, pair].astype(BF16)))
        qk_h = qk_pair[:, (h % 2) * L:(h % 2 + 1) * L]
        _mlstm_head(h, q_all, k_all, qk_h, a8, ct, causal, proj_ref, mlnw_ref, c_ref, n_ref, y_ref,
                    rows, fill)
        fill()
    fill.flush()


def _mixer_kernel(steps_per_seq,
                  x0_ref, xa_ref, xb_ref, nw_ref, w_ref, wg_ref,
                  convw_ref, convb_ref, gbias_ref, mlnw_ref, hglb_ref, hgnw_ref,
                  y_ref,
                  proj_a, proj_b, gate_a, gate_b, u_a, u_b,
                  qkext_ref, c_ref, n_ref, m_ref, st_ref, lv_ref):
    L = proj_a.shape[1]
    g = pl.program_id(0)

    @pl.when(g == 0)
    def _():
        _normalise(x0_ref[0:L, :], nw_ref, u_a)
        gate_a[...] = _dot_nt(wg_ref[...], u_a[...])

        def project_tile(j, carry):
            proj_a[j] = _dot(u_a[...], w_ref[j])
            return carry

        lax.fori_loop(0, w_ref.shape[0], project_tile, 0)
        _normalise(x0_ref[L:2 * L, :], nw_ref, u_b)
        r = lax.broadcasted_iota(jnp.int32, (L, L), 0)
        c = lax.broadcasted_iota(jnp.int32, (L, L), 1)
        lv = 31 - lax.clz(r ^ c)
        lv = jnp.where(r > c, lv, -1)
        lv_ref[...] = jnp.concatenate([lv, lv], axis=1)

    @pl.when(g % steps_per_seq == 0)
    def _():
        qkext_ref[0:SUBLANES, :] = jnp.zeros((SUBLANES, 2 * ML_QK), F32)
        c_ref[...] = jnp.zeros(c_ref.shape, F32)
        n_ref[...] = jnp.zeros(n_ref.shape, F32)
        m_ref[...] = jnp.zeros(m_ref.shape, F32)
        st_ref[...] = jnp.zeros(st_ref.shape, F32)

    params = (convw_ref, convb_ref, gbias_ref, mlnw_ref, hglb_ref, hgnw_ref)
    state = (qkext_ref, c_ref, n_ref, m_ref, st_ref, lv_ref)
    _mix(proj_a, gate_a, *params, y_ref, 0, *state,
         _projection_tasks(u_b, w_ref, wg_ref, proj_b, gate_b),
         lambda: _normalise(xa_ref[...], nw_ref, u_a))
    _mix(proj_b, gate_b, *params, y_ref, L, *state,
         _projection_tasks(u_a, w_ref, wg_ref, proj_a, gate_a),
         lambda: _normalise(xb_ref[...], nw_ref, u_b))


def _mixers(x2, norm_w, w_main, w_gate, conv_w, conv_b, gbias, ml_norm_w, hg_lb,
            hg_norm_w, chunk, chunks_per_seq):
    t = x2.shape[0]
    L = chunk
    nchunks = t // L
    steps = nchunks // 2
    assert chunks_per_seq % 2 == 0
    const = lambda shape: pl.BlockSpec(shape, lambda g: (0, 0), pipeline_mode=pl.Buffered(1))
    return pl.pallas_call(
        functools.partial(_mixer_kernel, chunks_per_seq // 2),
        grid=(steps,),
        in_specs=[
            const((2 * L, D_MODEL)),
            pl.BlockSpec((L, D_MODEL), lambda g: (jnp.minimum(2 * g + 2, nchunks - 1), 0)),
            pl.BlockSpec((L, D_MODEL), lambda g: (jnp.minimum(2 * g + 3, nchunks - 1), 0)),
            const((1, D_MODEL)),
            pl.BlockSpec(w_main.shape, lambda g: (0, 0, 0), pipeline_mode=pl.Buffered(1)),
            const((GATE_ROWS, D_MODEL)),
            const((ML_CONV, 2 * ML_QK)),
            const((1, 2 * ML_QK)),
            const((GATE_ROWS, LANES)),
            const((1, ML_WIDTH)),
            const((2, HG_WIDTH)),
            const((1, HG_WIDTH)),
        ],
        out_specs=pl.BlockSpec((2 * L, ML_WIDTH + HG_WIDTH), lambda g: (g, 0)),
        out_shape=jax.ShapeDtypeStruct((t, ML_WIDTH + HG_WIDTH), BF16),
        scratch_shapes=[
            pltpu.VMEM((MAIN_COLS // TN_IN, L, TN_IN), F32),
            pltpu.VMEM((MAIN_COLS // TN_IN, L, TN_IN), F32),
            pltpu.VMEM((GATE_ROWS, L), F32),
            pltpu.VMEM((GATE_ROWS, L), F32),
            pltpu.VMEM((L, D_MODEL), BF16),
            pltpu.VMEM((L, D_MODEL), BF16),
            pltpu.VMEM((SUBLANES, 2 * ML_QK), F32),
            pltpu.VMEM((ML_HEADS, ML_DQK, ML_DV), F32),
            pltpu.VMEM((ML_HEADS, SUBLANES, ML_DQK), F32),
            pltpu.VMEM((SUBLANES, LANES), F32),
            pltpu.VMEM((HG_HEADS // 2, HG_DV, 2 * HG_E), F32),
            pltpu.VMEM((L, 2 * L), jnp.int32),
        ],
        compiler_params=pltpu.CompilerParams(
            dimension_semantics=("arbitrary",),
            vmem_limit_bytes=V7X_VMEM_LIMIT),
        name="mixers",
    )(x2, x2, x2, norm_w, w_main, w_gate, conv_w, conv_b, gbias, ml_norm_w, hg_lb, hg_norm_w)


def _out_kernel(x_ref, y_ref, p_ref, wout_ref, wpg_ref, wpe_ref, pnw_ref, fnw_ref, o_ref):
    half = x_ref.shape[0] // OUT_SPLIT
    parts = [slice(i * half, (i + 1) * half) for i in range(OUT_SPLIT)]
    hs = [x_ref[rows, :] + _dot(y_ref[rows, :], wout_ref[...]) for rows in parts]
    for rows, h in zip(parts, hs):
        hn = h * _rms_scale(h) * pnw_ref[...]
        gate = _sigmoid(_dot(hn.astype(BF16), wpg_ref[...]))
        emb = _dot(p_ref[rows, :].astype(BF16), wpe_ref[...])
        h = h + gate * emb
        o_ref[rows, :] = h * _rms_scale(h) * fnw_ref[...]


def _out_stage(x2, y, p2, w_out, w_pg, w_pe, pe_norm_w, final_norm_w):
    t = x2.shape[0]
    tm = min(TM_OUT, t)
    const = lambda shape: pl.BlockSpec(shape, lambda i: (0, 0), pipeline_mode=pl.Buffered(1))
    return pl.pallas_call(
        _out_kernel,
        grid=(t // tm,),
        in_specs=[
            pl.BlockSpec((tm, D_MODEL), lambda i: (i, 0)),
            pl.BlockSpec((tm, D_MODEL), lambda i: (i, 0)),
            pl.BlockSpec((tm, PE_DIM), lambda i: (i, 0)),
            const(w_out.shape),
            const(w_pg.shape),
            const(w_pe.shape),
            const((1, D_MODEL)),
            const((1, D_MODEL)),
        ],
        out_specs=pl.BlockSpec((tm, D_MODEL), lambda i: (i, 0)),
        out_shape=jax.ShapeDtypeStruct((t, D_MODEL), F32),
        compiler_params=pltpu.CompilerParams(
            dimension_semantics=("arbitrary",),
            vmem_limit_bytes=V7X_VMEM_LIMIT),
        name="out_stage",
    )(x2, y, p2, w_out, w_pg, w_pe, pe_norm_w, final_norm_w)


def kernel(x, p, norm_w, w_in, conv_w, conv_b, ml_b_i, ml_b_f, ml_norm_w, hg_lb, hg_norm_w,
           w_out, pe_norm_w, w_pg, w_pe, final_norm_w):
    batch, seq, _ = x.shape
    t = batch * seq
    chunk = min(CHUNK, seq)
    x2 = x.reshape(t, D_MODEL).astype(F32)
    p2 = p.reshape(-1, PE_DIM).astype(F32)

    w = w_in[0]
    n_gate = 2 * ML_HEADS
    gate_lo = 2 * ML_QK + 3 * ML_WIDTH
    w_main = jnp.concatenate([w[:, :gate_lo], w[:, gate_lo + n_gate:]], axis=1).astype(BF16)
    w_main = _col_tiles(w_main, TN_IN)
    w_gate = jnp.pad(w[:, gate_lo:gate_lo + n_gate].T, ((0, GATE_ROWS - n_gate), (0, 0))).astype(BF16)
    gbias = jnp.pad(jnp.concatenate([ml_b_i[0], ml_b_f[0]]).astype(F32), (0, GATE_ROWS - n_gate))
    gbias = jnp.broadcast_to(gbias[:, None], (GATE_ROWS, LANES))
    nw = norm_w[0].reshape(1, D_MODEL).astype(F32)

    y = _mixers(x2, nw, w_main, w_gate,
                conv_w[0].astype(F32), conv_b[0].reshape(1, -1).astype(F32), gbias,
                ml_norm_w[0].reshape(1, -1).astype(F32), hg_lb.astype(F32),
                hg_norm_w[0].reshape(1, -1).astype(F32), chunk, seq // chunk)
    out = _out_stage(x2, y, p2, w_out[0].astype(BF16), w_pg[0].astype(BF16), w_pe[0].astype(BF16),
                     pe_norm_w[0].reshape(1, -1).astype(F32),
                     final_norm_w.reshape(1, -1).astype(F32))
    return out.reshape(batch, seq, D_MODEL).astype(x.dtype)
```

```python
import functools
import math

import jax
import jax.numpy as jnp
from jax import lax
from jax.experimental import pallas as pl
from jax.experimental.pallas import tpu as pltpu

F32 = jnp.float32
BF16 = jnp.bfloat16

D_MODEL = 2048
PE_DIM = 256
ML_HEADS = 4
ML_DQK = 128
ML_DV = 256
ML_QK = ML_HEADS * ML_DQK
ML_WIDTH = ML_HEADS * ML_DV
ML_CONV = 4
HG_HEADS = 8
HG_E = 128
HG_DV = 128
HG_WIDTH = HG_HEADS * HG_DV
EPS = 1e-6

MAIN_COLS = 8192
COL_QK = 0
COL_V = 1024
COL_O = 2048
COL_Z = 3072
COL_HQ = 4096
COL_HF = 5120
COL_HI = 6144
COL_HG = 7168
GATE_ROWS = 16

LANES = 128
SUBLANES = 8
NEG_BIG = -1e30
V7X_VMEM_LIMIT = 58 * 1024 * 1024

CHUNK = 128
TN_IN = 256
TM_OUT = 512
OUT_SPLIT = 2


def _dot(a, b):
    return jnp.dot(a, b, preferred_element_type=F32)


def _dot_nt(a, b):
    return lax.dot_general(a, b, (((1,), (1,)), ((), ())), preferred_element_type=F32)


def _dot_tn(a, b):
    return lax.dot_general(a, b, (((0,), (0,)), ((), ())), preferred_element_type=F32)


def _col_tiles(w, tn):
    k, n = w.shape
    return w.reshape(k, n // tn, tn).transpose(1, 0, 2)


def _rms_scale(x):
    return lax.rsqrt(jnp.mean(x * x, axis=-1, keepdims=True) + EPS)


def _scan_lanes(x, op, fill):
    n = x.shape[1]
    lane = lax.broadcasted_iota(jnp.int32, x.shape, 1)
    s = 1
    while s < n:
        shifted = pltpu.roll(x, s, axis=1)
        x = op(x, jnp.where(lane >= s, shifted, fill))
        s *= 2
    return x


def _sigmoid(x):
    return 1.0 / (1.0 + jnp.exp(-x))


def _silu(x):
    return x * _sigmoid(x)


def _normalise(x, nw_ref, u_ref):
    u_ref[...] = (x * _rms_scale(x) * nw_ref[...]).astype(BF16)


def _cols(slab_ref, c0, c1):
    tn = slab_ref.shape[2]
    parts = []
    while c0 < c1:
        t, off = divmod(c0, tn)
        n = min(c1 - c0, tn - off)
        parts.append(slab_ref[t, :, off:off + n])
        c0 += n
    return parts[0] if len(parts) == 1 else jnp.concatenate(parts, axis=1)


def _projection_tasks(u_ref, w_ref, wg_ref, proj_dst, gate_dst):
    n_tiles = w_ref.shape[0]

    def gates():
        gate_dst[...] = _dot_nt(wg_ref[...], u_ref[...])

    def tile(j):
        proj_dst[j] = _dot(u_ref[...], w_ref[j])

    return [gates] + [functools.partial(tile, j) for j in range(n_tiles)]


class _Filler:
    def __init__(self, tasks, slots):
        self._tasks = list(tasks)
        self._total = len(self._tasks)
        self._slots = slots
        self._used = 0

    def __call__(self, n=1):
        self._used += n
        due = -(-self._total * self._used // self._slots)
        while self._total - len(self._tasks) < due and self._tasks:
            self._tasks.pop(0)()

    def flush(self):
        while self._tasks:
            self._tasks.pop(0)()


def _mlstm_gates(gate_ref, gbias_ref, m_ref, L):
    i8 = gate_ref[0:SUBLANES, :] + gbias_ref[0:SUBLANES, 0:1]
    f8 = pltpu.roll(i8, ML_HEADS, axis=0)
    lf8 = jnp.minimum(f8, 0.0) - jnp.log1p(jnp.exp(-jnp.abs(f8)))
    b8 = _scan_lanes(lf8, jnp.add, 0.0)
    a8 = i8 - b8
    cm8 = _scan_lanes(a8, jnp.maximum, NEG_BIG)
    mprev8 = m_ref[:, 0:1]
    mm8 = jnp.maximum(mprev8, cm8)
    mlast8 = jnp.max(mm8, axis=1, keepdims=True)
    g8 = jnp.sum(lf8, axis=1, keepdims=True)
    m_ref[...] = jnp.broadcast_to(g8 + mlast8, m_ref.shape)
    table = jnp.concatenate(
        [mm8, b8, a8,
         jnp.broadcast_to(mlast8, (SUBLANES, L)),
         jnp.broadcast_to(mprev8, (SUBLANES, L)),
         jnp.zeros((LANES - 5 * SUBLANES, L), F32)], axis=0)
    return a8, table.T


def _mlstm_head(h, q_all, k_all, qk_h, a8, ct, causal, proj_ref, mlnw_ref, c_ref, n_ref, y_ref, rows,
                fill):
    mm_c = ct[:, h:h + 1]
    b_c = ct[:, 8 + h:9 + h]
    a_c = ct[:, 16 + h:17 + h]
    mlast_c = ct[:, 24 + h:25 + h]
    mprev_c = ct[:, 32 + h:33 + h]
    a_r = a8[h:h + 1, :]

    qh = q_all[:, h * ML_DQK:(h + 1) * ML_DQK]
    kh = k_all[:, h * ML_DQK:(h + 1) * ML_DQK]
    vb = _cols(proj_ref, COL_V + h * ML_DV, COL_V + (h + 1) * ML_DV).astype(BF16)

    decay = jnp.exp(jnp.where(causal, a_r - mm_c, NEG_BIG))
    scores = qk_h * decay
    w_inter = jnp.exp(mprev_c - mm_c)
    c_old = c_ref[h]
    n_old = n_ref[h]
    lhs = jnp.concatenate([scores.astype(BF16), (w_inter * qh).astype(BF16)], axis=1)
    num = _dot(lhs, jnp.concatenate([vb, c_old.astype(BF16)], axis=0))
    den = (jnp.sum(scores, axis=-1, keepdims=True)
           + w_inter * jnp.sum(qh * n_old[0:1, :], axis=-1, keepdims=True))
    hh = num / jnp.maximum(jnp.abs(den), jnp.exp(-(b_c + mm_c)))
    fill()

    wa_c = jnp.exp(a_c - mlast_c)
    ws_c = jnp.exp(mprev_c - mlast_c)
    kw = kh * wa_c
    c_ref[h] = ws_c[0:ML_DQK, :] * c_old + _dot_tn(kw.astype(BF16), vb)
    n_ref[h] = ws_c[0:SUBLANES, :] * n_old + jnp.sum(kw, axis=0, keepdims=True)

    hn = hh * _rms_scale(hh) * mlnw_ref[:, h * ML_DV:(h + 1) * ML_DV]
    o_gate = _sigmoid(_cols(proj_ref, COL_O + h * ML_DV, COL_O + (h + 1) * ML_DV))
    z = _cols(proj_ref, COL_Z + h * ML_DV, COL_Z + (h + 1) * ML_DV)
    y_ref[rows, h * ML_DV:(h + 1) * ML_DV] = (o_gate * hn * _silu(z)).astype(y_ref.dtype)


def _split_halves(a, m):
    n, w = a.shape
    a4 = a.reshape(n // (2 * m), 2, m, w)
    return a4[:, 0], a4[:, 1]


def _merge_halves(lo, hi):
    nb, m, w = lo.shape
    return jnp.stack([lo, hi], axis=1).reshape(2 * nb * m, w)


def _block_diag(a):
    z = jnp.zeros((a.shape[0], LANES), a.dtype)
    return jnp.concatenate([jnp.concatenate([a[:, :LANES], z], axis=1),
                            jnp.concatenate([z, a[:, LANES:]], axis=1)], axis=0)


def _hgrn2_pair(hp, lb_all, proj_ref, hgnw_ref, st_ref, lv_ref, y_ref, rows, fill, fill_levels):
    L = proj_ref.shape[1]
    nlev = int(math.log2(L))
    width = 2 * HG_E
    sl = slice(hp * width, (hp + 1) * width)
    lb = lb_all[:, sl]
    q = _cols(proj_ref, COL_HQ + hp * width, COL_HQ + (hp + 1) * width)
    v = _cols(proj_ref, COL_HI + hp * width, COL_HI + (hp + 1) * width)
    f_pre = _cols(proj_ref, COL_HF + hp * width, COL_HF + (hp + 1) * width)
    kk = (1.0 - lb) * _sigmoid(-f_pre)
    c = jnp.log(lb + (1.0 - lb) * _sigmoid(f_pre))

    att = None
    q8 = q.reshape(L // SUBLANES, SUBLANES, width)
    kk8 = kk.reshape(L // SUBLANES, SUBLANES, width)
    sub = lax.broadcasted_iota(jnp.int32, q8.shape, 1)
    for lev in range(nlev):
        m = 1 << lev
        if m < SUBLANES:
            c8 = c.reshape(L // SUBLANES, SUBLANES, width)
            upper = (sub & m) != 0
            tot = c8
            s = 1
            while s < m:
                tot = jnp.where((sub & s) != 0, tot, pltpu.roll(tot, SUBLANES - s, axis=1))
                s *= 2
            e = jnp.exp(jnp.where(upper, c8, tot - c8))
            zz = (jnp.where(upper, q8, kk8) * e).reshape(L, width).astype(BF16)
            p = jnp.where(lv_ref[...] == lev, _dot_nt(zz, _block_diag(zz)), 0.0)
            att = p if att is None else att + p
            c = (c8 + jnp.where(upper, pltpu.roll(tot, m, axis=1), 0.0)).reshape(L, width)
        else:
            c_lo, c_hi = _split_halves(c, m)
            tot_lo = jnp.broadcast_to(c_lo[:, m - 1:m, :], c_lo.shape)
            z_lo = _split_halves(kk, m)[0] * jnp.exp(tot_lo - c_lo)
            z_hi = _split_halves(q, m)[1] * jnp.exp(c_hi)
            zz = _merge_halves(z_lo, z_hi).astype(BF16)
            p = _dot_nt(z_hi.reshape(L // 2, width).astype(BF16), _block_diag(zz))
            lv_hi = _split_halves(lv_ref[...], m)[1].reshape(L // 2, 2 * L)
            p = jnp.where(lv_hi == lev, p, 0.0).reshape(L // (2 * m), m, 2 * L)
            att_lo, att_hi = _split_halves(att, m)
            att = _merge_halves(att_lo, att_hi + p)
            c = _merge_halves(c_lo, c_hi + tot_lo)
        if lev in fill_levels:
            fill()

    tot = jnp.broadcast_to(c[L - 1:L, :], c.shape)
    st_old = st_ref[hp]
    vb = v.astype(BF16)
    qk_diag = q * kk
    o = (_dot(att.astype(BF16), _block_diag(vb))
         + _dot_nt((q * jnp.exp(c)).astype(BF16), _block_diag(st_old.astype(BF16))))
    ke = (kk * jnp.exp(tot - c)).astype(BF16)
    upd = jnp.concatenate([_dot_tn(vb[:, :HG_DV], ke[:, :HG_E]),
                           _dot_tn(vb[:, HG_DV:], ke[:, HG_E:])], axis=1)
    st_ref[hp] = st_old * jnp.exp(tot[0:HG_DV, :]) + upd

    gz = _cols(proj_ref, COL_HG + hp * width, COL_HG + (hp + 1) * width)
    gate = _silu(gz)
    nw = hgnw_ref[:, sl]
    for i in range(2):
        hs = slice(i * HG_DV, (i + 1) * HG_DV)
        oi = o[:, hs] + jnp.sum(qk_diag[:, hs], axis=-1, keepdims=True) * v[:, hs]
        on = oi * _rms_scale(oi) * nw[:, hs]
        col = ML_WIDTH + hp * width + i * HG_DV
        y_ref[rows, col:col + HG_DV] = (on * gate[:, hs]).astype(y_ref.dtype)


def _mix(proj_ref, gate_ref, convw_ref, convb_ref, gbias_ref, mlnw_ref, hglb_ref, hgnw_ref,
         y_ref, r0, qkext_ref, c_ref, n_ref, m_ref, st_ref, lv_ref, tasks, late_task):
    L = proj_ref.shape[1]
    rows = slice(r0, r0 + L)
    fill = _Filler(tasks, slots=6 + 5 * (HG_HEADS // 2) + 2 * ML_HEADS)

    fill(2)
    a8, ct = _mlstm_gates(gate_ref, gbias_ref, m_ref, L)
    fill(2)

    cur = _cols(proj_ref, COL_QK, COL_QK + 2 * ML_QK)
    ext8 = jnp.concatenate([qkext_ref[0:SUBLANES, :], cur], axis=0).reshape(
        L // SUBLANES + 1, SUBLANES, 2 * ML_QK)
    srow = lax.broadcasted_iota(jnp.int32, (L // SUBLANES, SUBLANES, 2 * ML_QK), 1)
    conv = convb_ref[...] + ext8[1:] * convw_ref[ML_CONV - 1:ML_CONV, :]
    for d in range(1, ML_CONV):
        rot = pltpu.roll(ext8, d, axis=1)
        tap = ML_CONV - 1 - d
        conv = conv + jnp.where(srow >= d, rot[1:], rot[:-1]) * convw_ref[tap:tap + 1, :]
    conv = conv.reshape(L, 2 * ML_QK)
    qkext_ref[0:SUBLANES, :] = cur[L - SUBLANES:L, :]
    qk = _silu(conv)
    q_all = qk[:, :ML_QK]
    k_all = qk[:, ML_QK:] * (ML_DQK ** -0.5)
    fill(2)

    lbl = hglb_ref[...]
    lmax = jnp.max(lbl, axis=0, keepdims=True)
    lexp = jnp.exp(lbl - lmax)
    lb_all = lexp[0:1, :] / jnp.sum(lexp, axis=0, keepdims=True)
    for hp in range(HG_HEADS // 2):
        _hgrn2_pair(hp, lb_all, proj_ref, hgnw_ref, st_ref, lv_ref, y_ref, rows, fill,
                    fill_levels=(0, 1, 3, 4, 6))
        if hp == HG_HEADS // 4 - 1:
            late_task()

    rowi = lax.broadcasted_iota(jnp.int32, (L, L), 0)
    coli = lax.broadcasted_iota(jnp.int32, (L, L), 1)
    causal = coli <= rowi
    for h in range(ML_HEADS):
        if h % 2 == 0:
            pair = slice(h * ML_DQK, (h + 2) * ML_DQK)
            qk_pair = _dot_nt(q_all[:, pair].astype(BF16), _block_diag(k_all[:, pair].astype(BF16)))
        qk_h = qk_pair[:, (h % 2) * L:(h % 2 + 1) * L]
        _mlstm_head(h, q_all, k_all, qk_h, a8, ct, causal, proj_ref, mlnw_ref, c_ref, n_ref, y_ref,
                    rows, fill)
        fill()
    fill.flush()


def _mixer_kernel(steps_per_seq,
                  x0_ref, xa_ref, xb_ref, nw_ref, w_ref, wg_ref,
                  convw_ref, convb_ref, gbias_ref, mlnw_ref, hglb_ref, hgnw_ref,
                  y_ref,
                  proj_a, proj_b, gate_a, gate_b, u_a, u_b,
                  qkext_ref, c_ref, n_ref, m_ref, st_ref, lv_ref):
    L = proj_a.shape[1]
    g = pl.program_id(0)

    @pl.when(g == 0)
    def _():
        _normalise(x0_ref[0:L, :], nw_ref, u_a)
        gate_a[...] = _dot_nt(wg_ref[...], u_a[...])

        def project_tile(j, carry):
            proj_a[j] = _dot(u_a[...], w_ref[j])
            return carry

        lax.fori_loop(0, w_ref.shape[0], project_tile, 0)
        _normalise(x0_ref[L:2 * L, :], nw_ref, u_b)
        r = lax.broadcasted_iota(jnp.int32, (L, L), 0)
        c = lax.broadcasted_iota(jnp.int32, (L, L), 1)
        lv = 31 - lax.clz(r ^ c)
        lv = jnp.where(r > c, lv, -1)
        lv_ref[...] = jnp.concatenate([lv, lv], axis=1)

    @pl.when(g % steps_per_seq == 0)
    def _():
        qkext_ref[0:SUBLANES, :] = jnp.zeros((SUBLANES, 2 * ML_QK), F32)
        c_ref[...] = jnp.zeros(c_ref.shape, F32)
        n_ref[...] = jnp.zeros(n_ref.shape, F32)
        m_ref[...] = jnp.zeros(m_ref.shape, F32)
        st_ref[...] = jnp.zeros(st_ref.shape, F32)

    params = (convw_ref, convb_ref, gbias_ref, mlnw_ref, hglb_ref, hgnw_ref)
    state = (qkext_ref, c_ref, n_ref, m_ref, st_ref, lv_ref)
    _mix(proj_a, gate_a, *params, y_ref, 0, *state,
         _projection_tasks(u_b, w_ref, wg_ref, proj_b, gate_b),
         lambda: _normalise(xa_ref[...], nw_ref, u_a))
    _mix(proj_b, gate_b, *params, y_ref, L, *state,
         _projection_tasks(u_a, w_ref, wg_ref, proj_a, gate_a),
         lambda: _normalise(xb_ref[...], nw_ref, u_b))


def _mixers(x2, norm_w, w_main, w_gate, conv_w, conv_b, gbias, ml_norm_w, hg_lb,
            hg_norm_w, chunk, chunks_per_seq):
    t = x2.shape[0]
    L = chunk
    nchunks = t // L
    steps = nchunks // 2
    assert chunks_per_seq % 2 == 0
    const = lambda shape: pl.BlockSpec(shape, lambda g: (0, 0), pipeline_mode=pl.Buffered(1))
    return pl.pallas_call(
        functools.partial(_mixer_kernel, chunks_per_seq // 2),
        grid=(steps,),
        in_specs=[
            const((2 * L, D_MODEL)),
            pl.BlockSpec((L, D_MODEL), lambda g: (jnp.minimum(2 * g + 2, nchunks - 1), 0)),
            pl.BlockSpec((L, D_MODEL), lambda g: (jnp.minimum(2 * g + 3, nchunks - 1), 0)),
            const((1, D_MODEL)),
            pl.BlockSpec(w_main.shape, lambda g: (0, 0, 0), pipeline_mode=pl.Buffered(1)),
            const((GATE_ROWS, D_MODEL)),
            const((ML_CONV, 2 * ML_QK)),
            const((1, 2 * ML_QK)),
            const((GATE_ROWS, LANES)),
            const((1, ML_WIDTH)),
            const((2, HG_WIDTH)),
            const((1, HG_WIDTH)),
        ],
        out_specs=pl.BlockSpec((2 * L, ML_WIDTH + HG_WIDTH), lambda g: (g, 0)),
        out_shape=jax.ShapeDtypeStruct((t, ML_WIDTH + HG_WIDTH), BF16),
        scratch_shapes=[
            pltpu.VMEM((MAIN_COLS // TN_IN, L, TN_IN), F32),
            pltpu.VMEM((MAIN_COLS // TN_IN, L, TN_IN), F32),
            pltpu.VMEM((GATE_ROWS, L), F32),
            pltpu.VMEM((GATE_ROWS, L), F32),
            pltpu.VMEM((L, D_MODEL), BF16),
            pltpu.VMEM((L, D_MODEL), BF16),
            pltpu.VMEM((SUBLANES, 2 * ML_QK), F32),
            pltpu.VMEM((ML_HEADS, ML_DQK, ML_DV), F32),
            pltpu.VMEM((ML_HEADS, SUBLANES, ML_DQK), F32),
            pltpu.VMEM((SUBLANES, LANES), F32),
            pltpu.VMEM((HG_HEADS // 2, HG_DV, 2 * HG_E), F32),
            pltpu.VMEM((L, 2 * L), jnp.int32),
        ],
        compiler_params=pltpu.CompilerParams(
            dimension_semantics=("arbitrary",),
            vmem_limit_bytes=V7X_VMEM_LIMIT),
        name="mixers",
    )(x2, x2, x2, norm_w, w_main, w_gate, conv_w, conv_b, gbias, ml_norm_w, hg_lb, hg_norm_w)


def _out_kernel(x_ref, y_ref, p_ref, wout_ref, wpg_ref, wpe_ref, pnw_ref, fnw_ref, o_ref):
    half = x_ref.shape[0] // OUT_SPLIT
    parts = [slice(i * half, (i + 1) * half) for i in range(OUT_SPLIT)]
    hs = [x_ref[rows, :] + _dot(y_ref[rows, :], wout_ref[...]) for rows in parts]
    for rows, h in zip(parts, hs):
        hn = h * _rms_scale(h) * pnw_ref[...]
        gate = _sigmoid(_dot(hn.astype(BF16), wpg_ref[...]))
        emb = _dot(p_ref[rows, :].astype(BF16), wpe_ref[...])
        h = h + gate * emb
        o_ref[rows, :] = h * _rms_scale(h) * fnw_ref[...]


def _out_stage(x2, y, p2, w_out, w_pg, w_pe, pe_norm_w, final_norm_w):
    t = x2.shape[0]
    tm = min(TM_OUT, t)
    const = lambda shape: pl.BlockSpec(shape, lambda i: (0, 0), pipeline_mode=pl.Buffered(1))
    return pl.pallas_call(
        _out_kernel,
        grid=(t // tm,),
        in_specs=[
            pl.BlockSpec((tm, D_MODEL), lambda i: (i, 0)),
            pl.BlockSpec((tm, D_MODEL), lambda i: (i, 0)),
            pl.BlockSpec((tm, PE_DIM), lambda i: (i, 0)),
            const(w_out.shape),
            const(w_pg.shape),
            const(w_pe.shape),
            const((1, D_MODEL)),
            const((1, D_MODEL)),
        ],
        out_specs=pl.BlockSpec((tm, D_MODEL), lambda i: (i, 0)),
        out_shape=jax.ShapeDtypeStruct((t, D_MODEL), F32),
        compiler_params=pltpu.CompilerParams(
            dimension_semantics=("arbitrary",),
            vmem_limit_bytes=V7X_VMEM_LIMIT),
        name="out_stage",
    )(x2, y, p2, w_out, w_pg, w_pe, pe_norm_w, final_norm_w)


def kernel(x, p, norm_w, w_in, conv_w, conv_b, ml_b_i, ml_b_f, ml_norm_w, hg_lb, hg_norm_w,
           w_out, pe_norm_w, w_pg, w_pe, final_norm_w):
    batch, seq, _ = x.shape
    t = batch * seq
    chunk = min(CHUNK, seq)
    x2 = x.reshape(t, D_MODEL).astype(F32)
    p2 = p.reshape(-1, PE_DIM).astype(F32)

    w = w_in[0]
    n_gate = 2 * ML_HEADS
    gate_lo = 2 * ML_QK + 3 * ML_WIDTH
    w_main = jnp.concatenate([w[:, :gate_lo], w[:, gate_lo + n_gate:]], axis=1).astype(BF16)
    w_main = _col_tiles(w_main, TN_IN)
    w_gate = jnp.pad(w[:, gate_lo:gate_lo + n_gate].T, ((0, GATE_ROWS - n_gate), (0, 0))).astype(BF16)
    gbias = jnp.pad(jnp.concatenate([ml_b_i[0], ml_b_f[0]]).astype(F32), (0, GATE_ROWS - n_gate))
    gbias = jnp.broadcast_to(gbias[:, None], (GATE_ROWS, LANES))
    nw = norm_w[0].reshape(1, D_MODEL).astype(F32)

    y = _mixers(x2, nw, w_main, w_gate,
                conv_w[0].astype(F32), conv_b[0].reshape(1, -1).astype(F32), gbias,
                ml_norm_w[0].reshape(1, -1).astype(F32), hg_lb.astype(F32),
                hg_norm_w[0].reshape(1, -1).astype(F32), chunk, seq // chunk)
    out = _out_stage(x2, y, p2, w_out[0].astype(BF16), w_pg[0].astype(BF16), w_pe[0].astype(BF16),
                     pe_norm_w[0].reshape(1, -1).astype(F32),
                     final_norm_w.reshape(1, -1).astype(F32))
    return out.reshape(batch, seq, D_MODEL).astype(x.dtype)
```
